```python
import math
import jax, jax.numpy as jnp
from jax import lax
import numpy as np

D_MODEL = 2048
BATCH = 2
SEQ = 4096
DEPTH = 4

N_HEADS = 8
HEAD_DIM = 128
ATTN_W = N_HEADS * HEAD_DIM
MOBA_BLOCK = 256
MOBA_TOPK = 3
MOBA_QCHUNK = 64
N_BUCKETS = 32
MAX_DISTANCE = 128
LRU_W = D_MODEL // 2
LRU_BLOCKS = 8
LRU_BW = LRU_W // LRU_BLOCKS
LRU_CONV = 4
LRU_C = 8.0
SC_W = D_MODEL // 2
SC_CONV = 3
D_FF = 4 * D_MODEL
N_BRANCH = 3
EPS = 1e-6
IN_SPLITS = (ATTN_W, ATTN_W, ATTN_W, LRU_W, LRU_W, SC_W, SC_W, SC_W, N_BRANCH * D_MODEL)
N_IN = sum(IN_SPLITS)

kernel_name = "hybrid_moba_rglru_shortconv_trunk"


def rms_norm(x, g):
    xf = x.astype(jnp.float32)
    y = xf * lax.rsqrt(jnp.mean(xf * xf, axis=-1, keepdims=True) + EPS)
    return (y * g.astype(jnp.float32)).astype(x.dtype)


def causal_depthwise_conv(x, w):
    k = w.shape[0]
    return lax.conv_general_dilated(
        x, w[:, None, :].astype(x.dtype), window_strides=(1,), padding=[(k - 1, 0)],
        dimension_numbers=('NWC', 'WIO', 'NWC'), feature_group_count=x.shape[-1])


def t5_bucket(dist):
    n = jnp.maximum(dist, 0)
    max_exact = N_BUCKETS // 2
    nf = jnp.maximum(n, 1).astype(jnp.float32)
    large = max_exact + (jnp.log(nf / max_exact) / math.log(MAX_DISTANCE / max_exact)
                         * (N_BUCKETS - max_exact)).astype(jnp.int32)
    large = jnp.minimum(large, N_BUCKETS - 1)
    return jnp.where(n < max_exact, n, large)


def moba_attention(q, k, v, rel_table):
    bsz, seq, n_heads, d_head = q.shape
    n_blk = -(-seq // MOBA_BLOCK)
    s_pad = n_blk * MOBA_BLOCK
    topk = min(MOBA_TOPK, n_blk)
    pad = ((0, 0), (0, s_pad - seq), (0, 0), (0, 0))
    qh = jnp.pad(q * (d_head ** -0.5), pad).transpose(0, 2, 1, 3)
    kh = jnp.pad(k, pad).transpose(0, 2, 1, 3)
    vh = jnp.pad(v, pad).transpose(0, 2, 1, 3)
    kb = kh.reshape(bsz, n_heads, n_blk, MOBA_BLOCK, d_head)
    vb = vh.reshape(bsz, n_heads, n_blk, MOBA_BLOCK, d_head)
    k_mean = jnp.mean(kb.astype(jnp.float32), axis=3)
    gate = jnp.einsum('bhsd,bhnd->bhsn', qh.astype(jnp.float32), k_mean)
    q_blk = jnp.arange(s_pad) // MOBA_BLOCK
    past = jnp.arange(n_blk)[None, :] < q_blk[:, None]
    gate = jnp.where(past, gate, -jnp.inf)
    _, sel = lax.top_k(gate, topk)
    slot_ok = jnp.arange(topk)[None, :] < q_blk[:, None]
    b_idx = jnp.arange(bsz)[:, None, None, None]
    h_idx = jnp.arange(n_heads)[None, :, None, None]
    tbl_hb = rel_table.T
    blk_pos = jnp.arange(MOBA_BLOCK)

    def chunk(c):
        start = c * MOBA_QCHUNK
        qc = lax.dynamic_slice_in_dim(qh, start, MOBA_QCHUNK, axis=2)
        sel_c = lax.dynamic_slice_in_dim(sel, start, MOBA_QCHUNK, axis=2)
        ok_c = lax.dynamic_slice_in_dim(slot_ok, start, MOBA_QCHUNK, axis=0)
        q_pos = start + jnp.arange(MOBA_QCHUNK)
        kg = kb[b_idx, h_idx, sel_c]
        vg = vb[b_idx, h_idx, sel_c]
        s_past = jnp.einsum('bhqd,bhqnkd->bhqnk', qc, kg, preferred_element_type=jnp.float32)
        k_pos = sel_c[..., None] * MOBA_BLOCK + blk_pos
        bkt = t5_bucket(q_pos[:, None, None] - k_pos)
        s_past = s_past + tbl_hb[h_idx[..., None], bkt]
        s_past = jnp.where(ok_c[:, :, None], s_past, -jnp.inf)
        own = start // MOBA_BLOCK
        k_own = lax.dynamic_index_in_dim(kb, own, axis=2, keepdims=False)
        v_own = lax.dynamic_index_in_dim(vb, own, axis=2, keepdims=False)
        s_own = jnp.einsum('bhqd,bhkd->bhqk', qc, k_own, preferred_element_type=jnp.float32)
        dist = q_pos[:, None] - (own * MOBA_BLOCK + blk_pos)[None, :]
        s_own = s_own + rel_table[t5_bucket(dist)].transpose(2, 0, 1)
        s_own = jnp.where(dist >= 0, s_own, -jnp.inf)
        logits = jnp.concatenate(
            [s_past.reshape(bsz, n_heads, MOBA_QCHUNK, topk * MOBA_BLOCK), s_own], axis=-1)
        p = jax.nn.softmax(logits, axis=-1)
        p_past = p[..., :topk * MOBA_BLOCK].reshape(bsz, n_heads, MOBA_QCHUNK, topk, MOBA_BLOCK)
        p_own = p[..., topk * MOBA_BLOCK:]
        o = (jnp.einsum('bhqnk,bhqnkd->bhqd', p_past.astype(vg.dtype), vg,
                        preferred_element_type=jnp.float32)
             + jnp.einsum('bhqk,bhkd->bhqd', p_own.astype(v_own.dtype), v_own,
                          preferred_element_type=jnp.float32))
        return o.astype(q.dtype)

    out = lax.map(chunk, jnp.arange(s_pad // MOBA_QCHUNK))
    out = out.transpose(1, 0, 3, 2, 4).reshape(bsz, s_pad, n_heads * d_head)
    return out[:, :seq]


def _lru_combine(c1, c2):
    a1, b1 = c1
    a2, b2 = c2
    return a1 * a2, a2 * b1 + b2


def rg_lru_branch(xr, gate_br, conv_w, conv_b, wa, ba, wx, bx, lam):
    xc = causal_depthwise_conv(xr, conv_w) + conv_b
    bsz, seq, width = xc.shape
    xg = xc.reshape(bsz, seq, LRU_BLOCKS, LRU_BW)
    r = jax.nn.sigmoid((jnp.einsum('bsgi,gij->bsgj', xg, wa).reshape(bsz, seq, width) + ba)
                       .astype(jnp.float32))
    i = jax.nn.sigmoid((jnp.einsum('bsgi,gij->bsgj', xg, wx).reshape(bsz, seq, width) + bx)
                       .astype(jnp.float32))
    log_a = -LRU_C * r * jax.nn.softplus(-lam.astype(jnp.float32))
    a = jnp.exp(log_a)
    mult = jnp.sqrt(-jnp.expm1(2.0 * log_a))
    u = mult * i * xc.astype(jnp.float32)
    _, h = lax.associative_scan(_lru_combine, (a, u), axis=1)
    return (h * jax.nn.gelu(gate_br.astype(jnp.float32))).astype(xr.dtype)


def short_conv_branch(b, c, xs, conv_w):
    return b * causal_depthwise_conv(c * xs, conv_w)


def setup_inputs(seed: int = 0) -> dict:
    key = jax.random.key(seed)
    ks = jax.random.split(key, 24)
    f32 = jnp.float32
    nrm = lambda k, shape, fan_in: jax.random.normal(k, shape, f32) * (fan_in ** -0.5)
    u = jax.random.uniform(ks[12], (DEPTH, LRU_W), f32, minval=0.9, maxval=0.999)
    a_base = u ** (1.0 / LRU_C)
    lam = jnp.log(a_base) - jnp.log1p(-a_base)
    return {
        "x": jax.random.normal(ks[0], (BATCH, SEQ, D_MODEL), f32),
        "norm_mix_g": 1.0 + 0.02 * jax.random.normal(ks[1], (DEPTH, D_MODEL), f32),
        "w_in": nrm(ks[2], (DEPTH, D_MODEL, N_IN), D_MODEL),
        "gate_b": 0.02 * jax.random.normal(ks[3], (DEPTH, N_BRANCH * D_MODEL), f32),
        "rel_table": 0.1 * jax.random.normal(ks[4], (N_BUCKETS, N_HEADS), f32),
        "w_attn_o": nrm(ks[5], (DEPTH, ATTN_W, D_MODEL), ATTN_W),
        "lru_conv_w": nrm(ks[6], (DEPTH, LRU_CONV, LRU_W), LRU_CONV),
        "lru_conv_b": 0.02 * jax.random.normal(ks[7], (DEPTH, LRU_W), f32),
        "lru_wa": nrm(ks[8], (DEPTH, LRU_BLOCKS, LRU_BW, LRU_BW), LRU_BW),
        "lru_ba": 0.02 * jax.random.normal(ks[9], (DEPTH, LRU_W), f32),
        "lru_wx": nrm(ks[10], (DEPTH, LRU_BLOCKS, LRU_BW, LRU_BW), LRU_BW),
        "lru_bx": 0.02 * jax.random.normal(ks[11], (DEPTH, LRU_W), f32),
        "lru_lambda": lam,
        "w_lru_o": nrm(ks[13], (DEPTH, LRU_W, D_MODEL), LRU_W),
        "sc_conv_w": nrm(ks[14], (DEPTH, SC_CONV, SC_W), SC_CONV),
        "w_sc_o": nrm(ks[15], (DEPTH, SC_W, D_MODEL), SC_W),
        "w_out": nrm(ks[16], (DEPTH, D_MODEL, D_MODEL), D_MODEL),
        "norm_mlp_g": 1.0 + 0.02 * jax.random.normal(ks[17], (DEPTH, D_MODEL), f32),
        "w_mlp_up": nrm(ks[18], (DEPTH, D_MODEL, D_FF), D_MODEL),
        "w_mlp_down": nrm(ks[19], (DEPTH, D_FF, D_MODEL), D_FF),
        "final_g": 1.0 + 0.02 * jax.random.normal(ks[20], (D_MODEL,), f32),
    }


def reference(x, norm_mix_g, w_in, gate_b, rel_table, w_attn_o, lru_conv_w, lru_conv_b,
              lru_wa, lru_ba, lru_wx, lru_bx, lru_lambda, w_lru_o, sc_conv_w, w_sc_o,
              w_out, norm_mlp_g, w_mlp_up, w_mlp_down, final_g):
    bsz, seq, _ = x.shape
    offs = np.cumsum(IN_SPLITS)[:-1].tolist()
    for l in range(DEPTH):
        h = rms_norm(x, norm_mix_g[l])
        proj = h @ w_in[l]
        q, k, v, xr, gr, sc_b, sc_c, sc_x, g_pre = jnp.split(proj, offs, axis=-1)
        heads = (bsz, seq, N_HEADS, HEAD_DIM)
        y_a = moba_attention(q.reshape(heads), k.reshape(heads), v.reshape(heads),
                             rel_table) @ w_attn_o[l]
        y_b = rg_lru_branch(xr, gr, lru_conv_w[l], lru_conv_b[l], lru_wa[l], lru_ba[l],
                            lru_wx[l], lru_bx[l], lru_lambda[l]) @ w_lru_o[l]
        y_c = short_conv_branch(sc_b, sc_c, sc_x, sc_conv_w[l]) @ w_sc_o[l]
        g_a, g_b, g_c = jnp.split(jax.nn.sigmoid(g_pre + gate_b[l]), N_BRANCH, axis=-1)
        x = x + (g_a * y_a + g_b * y_b + g_c * y_c) @ w_out[l]
        h = rms_norm(x, norm_mlp_g[l])
        x = x + jnp.square(jax.nn.relu(h @ w_mlp_up[l])) @ w_mlp_down[l]
    return rms_norm(x, final_g)
```

```python
import functools
import math

import jax
import jax.numpy as jnp
from jax import lax
from jax.experimental import pallas as pl
from jax.experimental.pallas import tpu as pltpu

D_MODEL = 2048
DEPTH = 4
N_HEADS = 8
HEAD_DIM = 128
ATTN_W = N_HEADS * HEAD_DIM
MOBA_BLOCK = 256
MOBA_TOPK = 3
N_BUCKETS = 32
MAX_DISTANCE = 128
LRU_W = D_MODEL // 2
LRU_BLOCKS = 8
LRU_BW = LRU_W // LRU_BLOCKS
LRU_CONV = 4
LRU_C = 8.0
SC_W = D_MODEL // 2
SC_CONV = 3
D_FF = 4 * D_MODEL
N_BRANCH = 3
EPS = 1e-6
N_IN = 3 * ATTN_W + 2 * LRU_W + 3 * SC_W + N_BRANCH * D_MODEL

COL_XR, COL_GR, COL_SCB, COL_SCC, COL_SCX = 3, 4, 5, 6, 7
COL_GATES = 8

LANES = 128
SUBLANES = 8
MASK_NEG = -1e30
VMEM_LIMIT = 56 * 1024 * 1024

F32 = jnp.float32
BF16 = jnp.bfloat16
NT_DIMS = (((1,), (1,)), ((), ()))


def _params(*sem):
    return pltpu.CompilerParams(dimension_semantics=sem, vmem_limit_bytes=VMEM_LIMIT)


def _rms(x, g):
    ms = jnp.mean(x * x, axis=-1, keepdims=True)
    return x * lax.rsqrt(ms + EPS) * g


def _bias_kernel(tbl_ref, o_ref):
    h = pl.program_id(0)
    r = lax.broadcasted_iota(jnp.int32, (MOBA_BLOCK, MOBA_BLOCK), 0)
    c = lax.broadcasted_iota(jnp.int32, (MOBA_BLOCK, MOBA_BLOCK), 1)
    max_exact = N_BUCKETS // 2
    for t in range(2):
        dist = r - c + t * MOBA_BLOCK
        n = jnp.maximum(dist, 0)
        nf = jnp.maximum(n, 1).astype(F32)
        large = max_exact + (jnp.log(nf / max_exact) / math.log(MAX_DISTANCE / max_exact)
                             * (N_BUCKETS - max_exact)).astype(jnp.int32)
        large = jnp.minimum(large, N_BUCKETS - 1)
        bkt = jnp.where(n < max_exact, n, large)
        b = jnp.zeros((MOBA_BLOCK, MOBA_BLOCK), F32)
        for k in range(N_BUCKETS):
            b = jnp.where(bkt == k, tbl_ref[k, h], b)
        if t == 0:
            b = jnp.where(dist >= 0, b, MASK_NEG)
        o_ref[t] = b


def _bias_tiles(rel_table):
    return pl.pallas_call(
        _bias_kernel,
        out_shape=jax.ShapeDtypeStruct((N_HEADS, 2, MOBA_BLOCK, MOBA_BLOCK), F32),
        grid=(N_HEADS,),
        in_specs=[pl.BlockSpec(memory_space=pltpu.SMEM)],
        out_specs=pl.BlockSpec((None, 2, MOBA_BLOCK, MOBA_BLOCK), lambda h: (h, 0, 0, 0)),
        compiler_params=_params("arbitrary"),
        name="t5_bias_tiles",
    )(rel_table)


def _in_proj_kernel(x_ref, g_ref, w_ref, o_ref, h_ref):
    @pl.when(pl.program_id(1) == 0)
    def _():
        h_ref[...] = _rms(x_ref[...], g_ref[...]).astype(BF16)

    o_ref[...] = jnp.dot(h_ref[...], w_ref[...], preferred_element_type=F32).astype(o_ref.dtype)


def _in_proj(x2, g3, w_bf, layer):
    m_rows = x2.shape[0]
    tm, tn = 1024, 1024
    return pl.pallas_call(
        _in_proj_kernel,
        out_shape=jax.ShapeDtypeStruct((m_rows, N_IN), BF16),
        grid=(m_rows // tm, N_IN // tn),
        in_specs=[
            pl.BlockSpec((tm, D_MODEL), lambda m, n: (m, 0)),
            pl.BlockSpec((None, 1, D_MODEL), lambda m, n: (layer, 0, 0)),
            pl.BlockSpec((None, D_MODEL, tn), lambda m, n: (layer, 0, n)),
        ],
        out_specs=pl.BlockSpec((tm, tn), lambda m, n: (m, n)),
        scratch_shapes=[pltpu.VMEM((tm, D_MODEL), BF16)],
        compiler_params=_params("parallel", "arbitrary"),
        name="in_proj",
    )(x2, g3, w_bf)


def _attn_kernel(far_ref, q_ref, k_ref, v_ref, bias_ref, o_ref,
                 kmean_ref, qa_ref, m_ref, l_ref, acc_ref, *, n_blk):
    h = pl.program_id(1)
    qi = pl.program_id(2)
    blk = MOBA_BLOCK

    @pl.when(qi == 0)
    def _():
        for j in range(n_blk):
            kj = k_ref[j * blk:(j + 1) * blk, :].astype(F32)
            kmean_ref[j:j + 1, :] = jnp.sum(kj, axis=0, keepdims=True) * (1.0 / blk)

    qf = q_ref[...].astype(F32) * (HEAD_DIM ** -0.5)
    qb = qf.astype(BF16)

    gate_t = lax.dot_general(kmean_ref[...], qf, NT_DIMS, precision=lax.Precision.HIGHEST,
                             preferred_element_type=F32)
    row = lax.broadcasted_iota(jnp.int32, (n_blk, blk), 0)
    rank = jnp.zeros((n_blk, blk), F32)
    for i in range(n_blk):
        gi = gate_t[i:i + 1, :]
        tie_first = jnp.where(row > i, 1.0, 0.0)
        beats = jnp.where(gi > gate_t, 1.0, jnp.where(gi == gate_t, tie_first, 0.0))
        rank = rank + beats * jnp.where(i < qi, 1.0, 0.0)
    neg_t = jnp.where(rank < float(MOBA_TOPK), 0.0, MASK_NEG).astype(BF16)
    neg_t = jnp.concatenate([neg_t, jnp.zeros((LANES - n_blk, blk), BF16)], axis=0)
    eye = jnp.where(lax.broadcasted_iota(jnp.int32, (blk, blk), 0)
                    == lax.broadcasted_iota(jnp.int32, (blk, blk), 1), 1.0, 0.0).astype(BF16)
    neg_q = lax.dot_general(eye, neg_t, NT_DIMS, preferred_element_type=F32)
    qa_ref[:, 0:HEAD_DIM] = qb
    qa_ref[:, HEAD_DIM:2 * HEAD_DIM] = neg_q.astype(BF16)

    own = pl.multiple_of(qi * blk, blk)
    s = lax.dot_general(qb, k_ref[pl.ds(own, blk), :], NT_DIMS, preferred_element_type=F32)
    s = s + bias_ref[0]
    m0 = jnp.max(s, axis=1, keepdims=True)
    p = jnp.exp(s - m0)
    m_ref[...] = m0
    l_ref[...] = jnp.sum(p, axis=1, keepdims=True)
    acc_ref[...] = jnp.dot(p.astype(BF16), v_ref[pl.ds(own, blk), :], preferred_element_type=F32)

    lane = lax.broadcasted_iota(jnp.int32, (blk, LANES), 1)

    def past_block(j, bias):
        start = pl.multiple_of(j * blk, blk)
        onehot = jnp.where(lane == j, 1.0, 0.0).astype(BF16)
        ka = jnp.concatenate([k_ref[pl.ds(start, blk), :], onehot], axis=1)
        s = lax.dot_general(qa_ref[...], ka, NT_DIMS, preferred_element_type=F32) + bias
        m_prev = m_ref[...]
        m_new = jnp.maximum(m_prev, jnp.max(s, axis=1, keepdims=True))
        alpha = jnp.exp(m_prev - m_new)
        p = jnp.exp(s - m_new)
        l_ref[...] = alpha * l_ref[...] + jnp.sum(p, axis=1, keepdims=True)
        acc_ref[...] = alpha * acc_ref[...] + jnp.dot(
            p.astype(BF16), v_ref[pl.ds(start, blk), :], preferred_element_type=F32)
        m_ref[...] = m_new

    far_bias = far_ref[h]

    def far_body(j, carry):
        past_block(j, far_bias)
        return carry

    lax.fori_loop(0, qi - 1, far_body, 0)

    @pl.when(qi >= 1)
    def _():
        past_block(qi - 1, bias_ref[1])

    o_ref[...] = (acc_ref[...] / l_ref[...]).astype(o_ref.dtype)


def _attention(proj3, bias_tiles, far_bias):
    bsz, seq, _ = proj3.shape
    n_blk = seq // MOBA_BLOCK
    blk = MOBA_BLOCK
    return pl.pallas_call(
        functools.partial(_attn_kernel, n_blk=n_blk),
        out_shape=jax.ShapeDtypeStruct((bsz, seq, ATTN_W), BF16),
        grid=(bsz, N_HEADS, n_blk),
        in_specs=[
            pl.BlockSpec(memory_space=pltpu.SMEM),
            pl.BlockSpec((None, blk, HEAD_DIM), lambda b, h, i: (b, i, h)),
            pl.BlockSpec((None, seq, HEAD_DIM), lambda b, h, i: (b, 0, N_HEADS + h)),
            pl.BlockSpec((None, seq, HEAD_DIM), lambda b, h, i: (b, 0, 2 * N_HEADS + h)),
            pl.BlockSpec((None, 2, blk, blk), lambda b, h, i: (h, 0, 0, 0)),
        ],
        out_specs=pl.BlockSpec((None, blk, HEAD_DIM), lambda b, h, i: (b, i, h)),
        scratch_shapes=[
            pltpu.VMEM((n_blk, HEAD_DIM), F32),
            pltpu.VMEM((blk, 2 * HEAD_DIM), BF16),
            pltpu.VMEM((blk, 1), F32),
            pltpu.VMEM((blk, 1), F32),
            pltpu.VMEM((blk, HEAD_DIM), F32),
        ],
        compiler_params=_params("parallel", "parallel", "arbitrary"),
        name="moba_attention",
    )(far_bias, proj3, proj3, proj3, bias_tiles)


def _gelu_tanh(x):
    return 0.5 * x * (1.0 + jnp.tanh(math.sqrt(2.0 / math.pi) * (x + 0.044715 * (x * x * x))))


def _softplus(z):
    return jnp.maximum(z, 0.0) + jnp.log1p(jnp.exp(-jnp.abs(z)))


def _mix_kernel(xr_ref, gr_ref, scb_ref, scc_ref, scx_ref,
                cw_ref, cb_ref, wg_ref, ba_ref, bx_ref, lam_ref, sw_ref,
                yb_ref, yc_ref,
                xbuf, pbuf, a_s, u_s, hcar, *, ts):
    halo = SUBLANES
    first = pl.program_id(1) == 0

    @pl.when(first)
    def _():
        xbuf[0:halo, :] = jnp.zeros((halo, LRU_W), F32)
        pbuf[0:halo, :] = jnp.zeros((halo, SC_W), F32)
        hcar[...] = jnp.zeros_like(hcar)

    @pl.when(jnp.logical_not(first))
    def _():
        xbuf[0:halo, :] = xbuf[ts:ts + halo, :]
        pbuf[0:halo, :] = pbuf[ts:ts + halo, :]

    xbuf[halo:halo + ts, :] = xr_ref[...].astype(F32)
    xc = cb_ref[...] + cw_ref[LRU_CONV - 1:LRU_CONV, :] * xbuf[halo:halo + ts, :]
    for d in range(1, LRU_CONV):
        xc = xc + cw_ref[LRU_CONV - 1 - d:LRU_CONV - d, :] * xbuf[halo - d:halo - d + ts, :]
    xcb = xc.astype(BF16)
    neg_c_sp = -LRU_C * _softplus(-lam_ref[...])
    for g in range(LRU_BLOCKS):
        sl = slice(g * LRU_BW, (g + 1) * LRU_BW)
        ri = jnp.dot(xcb[:, sl], wg_ref[g], preferred_element_type=F32)
        r = jax.nn.sigmoid(ri[:, :LRU_BW] + ba_ref[:, sl])
        i = jax.nn.sigmoid(ri[:, LRU_BW:] + bx_ref[:, sl])
        log_a = neg_c_sp[:, sl] * r
        a = jnp.exp(log_a)
        mult = jnp.sqrt(-jnp.tanh(log_a) * (a * a + 1.0))
        a_s[:, sl] = a
        u_s[:, sl] = mult * i * xc[:, sl]

    rowi = lax.broadcasted_iota(jnp.int32, (SUBLANES, LRU_W), 0)

    def scan_body(t, hprev):
        r0 = pl.multiple_of(t * SUBLANES, SUBLANES)
        a = a_s[pl.ds(r0, SUBLANES), :]
        u = u_s[pl.ds(r0, SUBLANES), :]
        for sh in (1, 2, 4):
            a_sh = pltpu.roll(a, sh, axis=0)
            u_sh = pltpu.roll(u, sh, axis=0)
            ok = rowi >= sh
            u = jnp.where(ok, a * u_sh + u, u)
            a = jnp.where(ok, a * a_sh, a)
        hrow = a * hprev + u
        u_s[pl.ds(r0, SUBLANES), :] = hrow
        return jnp.broadcast_to(hrow[SUBLANES - 1:SUBLANES, :], (SUBLANES, LRU_W))

    hcar[...] = lax.fori_loop(0, ts // SUBLANES, scan_body, hcar[...], unroll=4)
    yb_ref[...] = (u_s[...] * _gelu_tanh(gr_ref[...].astype(F32))).astype(yb_ref.dtype)

    pbuf[halo:halo + ts, :] = scc_ref[...].astype(F32) * scx_ref[...].astype(F32)
    conv = sw_ref[SC_CONV - 1:SC_CONV, :] * pbuf[halo:halo + ts, :]
    for d in range(1, SC_CONV):
        conv = conv + sw_ref[SC_CONV - 1 - d:SC_CONV - d, :] * pbuf[halo - d:halo - d + ts, :]
    yc_ref[...] = (scb_ref[...].astype(F32) * conv).astype(yc_ref.dtype)


def _mixers(proj3, cw, cb3, wg_bf, ba3, bx3, lam3, sw, layer):
    bsz, seq, _ = proj3.shape
    ts = 512

    def col(c):
        return pl.BlockSpec((None, ts, LRU_W), lambda b, s: (b, s, c))

    def vec(width):
        return pl.BlockSpec((None, 1, width), lambda b, s: (layer, 0, 0))

    return pl.pallas_call(
        functools.partial(_mix_kernel, ts=ts),
        out_shape=(jax.ShapeDtypeStruct((bsz, seq, LRU_W), BF16),
                   jax.ShapeDtypeStruct((bsz, seq, SC_W), BF16)),
        grid=(bsz, seq // ts),
        in_specs=[
            col(COL_XR), col(COL_GR), col(COL_SCB), col(COL_SCC), col(COL_SCX),
            pl.BlockSpec((None, LRU_CONV, LRU_W), lambda b, s: (layer, 0, 0)),
            vec(LRU_W),
            pl.BlockSpec((None, LRU_BLOCKS, LRU_BW, 2 * LRU_BW), lambda b, s: (layer, 0, 0, 0)),
            vec(LRU_W), vec(LRU_W), vec(LRU_W),
            pl.BlockSpec((None, SC_CONV, SC_W), lambda b, s: (layer, 0, 0)),
        ],
        out_specs=(pl.BlockSpec((None, ts, LRU_W), lambda b, s: (b, s, 0)),
                   pl.BlockSpec((None, ts, SC_W), lambda b, s: (b, s, 0))),
        scratch_shapes=[
            pltpu.VMEM((SUBLANES + ts, LRU_W), F32),
            pltpu.VMEM((SUBLANES + ts, SC_W), F32),
            pltpu.VMEM((ts, LRU_W), F32),
            pltpu.VMEM((ts, LRU_W), F32),
            pltpu.VMEM((SUBLANES, LRU_W), F32),
        ],
        compiler_params=_params("parallel", "arbitrary"),
        name="lru_shortconv",
    )(proj3, proj3, proj3, proj3, proj3, cw, cb3, wg_bf, ba3, bx3, lam3, sw)


def _merge_kernel(ya_ref, yb_ref, yc_ref, ga_ref, gb_ref, gc_ref, gbias_ref,
                  wa_ref, wb_ref, wc_ref, o_ref):
    def branch(y_ref, w_ref, g_ref, k):
        y = jnp.dot(y_ref[...], w_ref[...], preferred_element_type=F32)
        gate = jax.nn.sigmoid(g_ref[...].astype(F32) + gbias_ref[:, k * D_MODEL:(k + 1) * D_MODEL])
        return gate * y

    m = branch(ya_ref, wa_ref, ga_ref, 0) + branch(yb_ref, wb_ref, gb_ref, 1)
    m = m + branch(yc_ref, wc_ref, gc_ref, 2)
    o_ref[...] = m.astype(o_ref.dtype)


def _merge(ya, yb, yc, proj, gbias3, wa_bf, wb_bf, wc_bf, layer):
    m_rows = ya.shape[0]
    tm = 256
    gate_tile = COL_GATES * 1024 // D_MODEL

    def rows(width):
        return pl.BlockSpec((tm, width), lambda m: (m, 0))

    def gate(k):
        return pl.BlockSpec((tm, D_MODEL), lambda m: (m, gate_tile + k))

    def weight(width):
        return pl.BlockSpec((None, width, D_MODEL), lambda m: (layer, 0, 0))

    return pl.pallas_call(
        _merge_kernel,
        out_shape=jax.ShapeDtypeStruct((m_rows, D_MODEL), BF16),
        grid=(m_rows // tm,),
        in_specs=[
            rows(ATTN_W), rows(LRU_W), rows(SC_W), gate(0), gate(1), gate(2),
            pl.BlockSpec((None, 1, N_BRANCH * D_MODEL), lambda m: (layer, 0, 0)),
            weight(ATTN_W), weight(LRU_W), weight(SC_W),
        ],
        out_specs=rows(D_MODEL),
        compiler_params=_params("parallel"),
        name="gated_merge",
    )(ya, yb, yc, proj, proj, proj, gbias3, wa_bf, wb_bf, wc_bf)


def _out_proj_kernel(m_ref, x_ref, w_ref, o_ref):
    o_ref[...] = x_ref[...] + jnp.dot(m_ref[...], w_ref[...], preferred_element_type=F32)


def _out_proj(m, x2, w_bf, layer):
    m_rows = x2.shape[0]
    tm = 512
    return pl.pallas_call(
        _out_proj_kernel,
        out_shape=jax.ShapeDtypeStruct((m_rows, D_MODEL), F32),
        grid=(m_rows // tm,),
        in_specs=[
            pl.BlockSpec((tm, D_MODEL), lambda i: (i, 0)),
            pl.BlockSpec((tm, D_MODEL), lambda i: (i, 0)),
            pl.BlockSpec((None, D_MODEL, D_MODEL), lambda i: (layer, 0, 0)),
        ],
        out_specs=pl.BlockSpec((tm, D_MODEL), lambda i: (i, 0)),
        compiler_params=_params("parallel"),
        name="out_proj",
    )(m, x2, w_bf)


def _mlp_kernel(x_ref, g_ref, wu_ref, wd_ref, fg_ref, o_ref, h_ref, acc_ref, *, final_norm):
    f = pl.program_id(1)

    @pl.when(f == 0)
    def _():
        h_ref[...] = _rms(x_ref[...], g_ref[...]).astype(BF16)
        acc_ref[...] = jnp.zeros_like(acc_ref)

    up = jnp.dot(h_ref[...], wu_ref[...], preferred_element_type=F32)
    act = jnp.square(jnp.maximum(up, 0.0)).astype(BF16)
    acc_ref[...] += jnp.dot(act, wd_ref[...], preferred_element_type=F32)

    @pl.when(f == pl.num_programs(1) - 1)
    def _():
        y = x_ref[...] + acc_ref[...]
        if final_norm:
            y = _rms(y, fg_ref[...])
        o_ref[...] = y


def _mlp(x2, g3, wu_bf, wd_bf, final_g2, layer, final_norm):
    m_rows = x2.shape[0]
    tm, tf = 512, 1024
    return pl.pallas_call(
        functools.partial(_mlp_kernel, final_norm=final_norm),
        out_shape=jax.ShapeDtypeStruct((m_rows, D_MODEL), F32),
        grid=(m_rows // tm, D_FF // tf),
        in_specs=[
            pl.BlockSpec((tm, D_MODEL), lambda m, f: (m, 0)),
            pl.BlockSpec((None, 1, D_MODEL), lambda m, f: (layer, 0, 0)),
            pl.BlockSpec((None, D_MODEL, tf), lambda m, f: (layer, 0, f)),
            pl.BlockSpec((None, tf, D_MODEL), lambda m, f: (layer, f, 0)),
            pl.BlockSpec((1, D_MODEL), lambda m, f: (0, 0)),
        ],
        out_specs=pl.BlockSpec((tm, D_MODEL), lambda m, f: (m, 0)),
        scratch_shapes=[pltpu.VMEM((tm, D_MODEL), BF16), pltpu.VMEM((tm, D_MODEL), F32)],
        compiler_params=_params("parallel", "arbitrary"),
        name="mlp",
    )(x2, g3, wu_bf, wd_bf, final_g2)


def kernel(x, norm_mix_g, w_in, gate_b, rel_table, w_attn_o, lru_conv_w, lru_conv_b,
           lru_wa, lru_ba, lru_wx, lru_bx, lru_lambda, w_lru_o, sc_conv_w, w_sc_o,
           w_out, norm_mlp_g, w_mlp_up, w_mlp_down, final_g):
    bsz, seq, d_model = x.shape
    assert d_model == D_MODEL and seq % MOBA_BLOCK == 0
    assert w_in.shape == (DEPTH, D_MODEL, N_IN)
    m_rows = bsz * seq

    def vec3(v):
        return v.reshape(v.shape[0], 1, v.shape[1])

    w_in_bf = w_in.astype(BF16)
    w_attn_bf = w_attn_o.astype(BF16)
    w_lru_bf = w_lru_o.astype(BF16)
    w_sc_bf = w_sc_o.astype(BF16)
    w_out_bf = w_out.astype(BF16)
    w_up_bf = w_mlp_up.astype(BF16)
    w_down_bf = w_mlp_down.astype(BF16)
    w_gates_bf = jnp.concatenate([lru_wa, lru_wx], axis=-1).astype(BF16)

    bias_tiles = _bias_tiles(rel_table)
    far_bias = rel_table[N_BUCKETS - 1]
    final_g2 = final_g.reshape(1, D_MODEL)

    x2 = x.reshape(m_rows, D_MODEL)
    for layer in range(DEPTH):
        proj = _in_proj(x2, vec3(norm_mix_g), w_in_bf, layer)
        proj3 = proj.reshape(bsz, seq, N_IN)
        ya = _attention(proj3, bias_tiles, far_bias)
        yb, yc = _mixers(proj3, lru_conv_w, vec3(lru_conv_b), w_gates_bf, vec3(lru_ba),
                         vec3(lru_bx), vec3(lru_lambda), sc_conv_w, layer)
        merged = _merge(ya.reshape(m_rows, ATTN_W), yb.reshape(m_rows, LRU_W),
                        yc.reshape(m_rows, SC_W), proj, vec3(gate_b),
                        w_attn_bf, w_lru_bf, w_sc_bf, layer)
        x2 = _out_proj(merged, x2, w_out_bf, layer)
        x2 = _mlp(x2, vec3(norm_mlp_g), w_up_bf, w_down_bf, final_g2, layer,
                  final_norm=(layer == DEPTH - 1))
    return x2.reshape(bsz, seq, D_MODEL)
```

```python
import functools
import math

import jax
import jax.numpy as jnp
from jax import lax
from jax.experimental import pallas as pl
from jax.experimental.pallas import tpu as pltpu

D_MODEL = 2048
DEPTH = 4
N_HEADS = 8
HEAD_DIM = 128
ATTN_W = N_HEADS * HEAD_DIM
MOBA_BLOCK = 256
MOBA_TOPK = 3
N_BUCKETS = 32
MAX_DISTANCE = 128
LRU_W = D_MODEL // 2
LRU_BLOCKS = 8
LRU_BW = LRU_W // LRU_BLOCKS
LRU_CONV = 4
LRU_C = 8.0
SC_W = D_MODEL // 2
SC_CONV = 3
D_FF = 4 * D_MODEL
N_BRANCH = 3
EPS = 1e-6
N_IN = 3 * ATTN_W + 2 * LRU_W + 3 * SC_W + N_BRANCH * D_MODEL

COL_XR, COL_GR, COL_SCB, COL_SCC, COL_SCX = 3, 4, 5, 6, 7
COL_GATES = 8

LANES = 128
SUBLANES = 8
MASK_NEG = -1e30
LOG2E = math.log2(math.e)
N_FAR_PARTS = 3
VMEM_LIMIT = 56 * 1024 * 1024

F32 = jnp.float32
BF16 = jnp.bfloat16
NT_DIMS = (((1,), (1,)), ((), ()))


def _params(*sem):
    return pltpu.CompilerParams(dimension_semantics=sem, vmem_limit_bytes=VMEM_LIMIT)


def _rms(x, g):
    ms = jnp.mean(x * x, axis=-1, keepdims=True)
    return x * lax.rsqrt(ms + EPS) * g


N_BIAS_TILES = 4


def _bias_tile_index(q_blk, k_blk):
    return jnp.clip(q_blk - k_blk, -1, N_BIAS_TILES - 2) + 1


def _bias_kernel(tbl_ref, o_ref):
    h = pl.program_id(0)
    far = tbl_ref[N_BUCKETS - 1, h]
    r = lax.broadcasted_iota(jnp.int32, (MOBA_BLOCK, MOBA_BLOCK), 0)
    c = lax.broadcasted_iota(jnp.int32, (MOBA_BLOCK, MOBA_BLOCK), 1)
    max_exact = N_BUCKETS // 2
    o_ref[0] = jnp.full((MOBA_BLOCK, MOBA_BLOCK), MASK_NEG, F32)
    o_ref[3] = jnp.zeros((MOBA_BLOCK, MOBA_BLOCK), F32)
    for t in range(2):
        dist = r - c + t * MOBA_BLOCK
        n = jnp.maximum(dist, 0)
        nf = jnp.maximum(n, 1).astype(F32)
        large = max_exact + (jnp.log(nf / max_exact) / math.log(MAX_DISTANCE / max_exact)
                             * (N_BUCKETS - max_exact)).astype(jnp.int32)
        large = jnp.minimum(large, N_BUCKETS - 1)
        bkt = jnp.where(n < max_exact, n, large)
        b = jnp.zeros((MOBA_BLOCK, MOBA_BLOCK), F32)
        for k in range(N_BUCKETS):
            b = jnp.where(bkt == k, tbl_ref[k, h], b)
        b = (b - far) * LOG2E
        if t == 0:
            b = jnp.where(dist >= 0, b, MASK_NEG)
        o_ref[1 + t] = b


def _bias_tiles(rel_table):
    tile_shape = (N_BIAS_TILES, MOBA_BLOCK, MOBA_BLOCK)
    return pl.pallas_call(
        _bias_kernel,
        out_shape=jax.ShapeDtypeStruct((N_HEADS,) + tile_shape, F32),
        grid=(N_HEADS,),
        in_specs=[pl.BlockSpec(memory_space=pltpu.SMEM)],
        out_specs=pl.BlockSpec((None,) + tile_shape, lambda h: (h, 0, 0, 0)),
        compiler_params=_params("arbitrary"),
        name="t5_bias_tiles",
    )(rel_table)


def _norm_kernel(x_ref, g_ref, o_ref):
    o_ref[...] = _rms(x_ref[...], g_ref[...]).astype(o_ref.dtype)


def _norm(x2, g3, layer):
    m_rows = x2.shape[0]
    tm = 512
    return pl.pallas_call(
        _norm_kernel,
        out_shape=jax.ShapeDtypeStruct((m_rows, D_MODEL), BF16),
        grid=(m_rows // tm,),
        in_specs=[pl.BlockSpec((tm, D_MODEL), lambda m: (m, 0)),
                  pl.BlockSpec((None, 1, D_MODEL), lambda m: (layer, 0, 0))],
        out_specs=pl.BlockSpec((tm, D_MODEL), lambda m: (m, 0)),
        compiler_params=_params("parallel"),
        name="rms_norm",
    )(x2, g3)


def _stationary_weight_matmul(w_ref, wbf_ref, lhs):
    @pl.when(pl.program_id(1) == 0)
    def _():
        wbf_ref[...] = w_ref[...].astype(BF16)

    return jnp.dot(lhs, wbf_ref[...], preferred_element_type=F32)


def _in_proj_kernel(h_ref, w_ref, o_ref, wbf_ref):
    o_ref[...] = _stationary_weight_matmul(w_ref, wbf_ref, h_ref[...]).astype(o_ref.dtype)


def _in_proj(h, w_in, layer):
    m_rows = h.shape[0]
    tm, tn = 1024, 1024
    return pl.pallas_call(
        _in_proj_kernel,
        out_shape=jax.ShapeDtypeStruct((m_rows, N_IN), BF16),
        grid=(N_IN // tn, m_rows // tm),
        in_specs=[
            pl.BlockSpec((tm, D_MODEL), lambda n, m: (m, 0)),
            pl.BlockSpec((None, D_MODEL, tn), lambda n, m: (layer, 0, n)),
        ],
        out_specs=pl.BlockSpec((tm, tn), lambda n, m: (m, n)),
        scratch_shapes=[pltpu.VMEM((D_MODEL, tn), BF16)],
        compiler_params=_params("parallel", "arbitrary"),
        name="in_proj",
    )(h, w_in)


ATTN_GROUP = 4
ATTN_Q_BLOCKS = 2


def _split_bf16(x, parts):
    pieces = []
    for _ in range(parts):
        piece = x.astype(BF16).astype(F32)
        pieces.append(piece)
        x = x - piece
    return pieces


def _attn_kernel(far_ref, q_ref, k_ref, v_ref, bias_ref, o_ref,
                 kmean_ref, qa_ref, s_ref, m128_ref, mb_ref, acc_ref, *, n_blk):
    h = pl.program_id(1)
    blk = MOBA_BLOCK
    q_rows = ATTN_Q_BLOCKS * blk
    far_lanes = (n_blk, n_blk + N_FAR_PARTS)
    assert n_blk % SUBLANES == 0 and N_FAR_PARTS <= SUBLANES and n_blk + SUBLANES <= LANES
    assert n_blk % ATTN_GROUP == 0 and n_blk % ATTN_Q_BLOCKS == 0

    for j in range(n_blk):
        kj = k_ref[j * blk:(j + 1) * blk, :].astype(F32)
        kmean_ref[j:j + 1, :] = jnp.sum(kj, axis=0, keepdims=True) * (1.0 / blk)
    kmean_parts = jnp.concatenate(_split_bf16(kmean_ref[...], 3), axis=0).astype(BF16)

    far = jnp.full((SUBLANES, blk), far_ref[h], F32) * LOG2E
    row8 = lax.broadcasted_iota(jnp.int32, (SUBLANES, blk), 0)
    far_rows = jnp.zeros((SUBLANES, blk), F32)
    for i, piece in enumerate(_split_bf16(far, N_FAR_PARTS)):
        far_rows = jnp.where(row8 == i, piece, far_rows)

    eye = jnp.where(lax.broadcasted_iota(jnp.int32, (blk, blk), 0)
                    == lax.broadcasted_iota(jnp.int32, (blk, blk), 1), 1.0, 0.0).astype(BF16)
    row = lax.broadcasted_iota(jnp.int32, (n_blk, blk), 0)
    pad_rows = jnp.zeros((LANES - n_blk - SUBLANES, blk), F32)

    for c in range(n_blk):
        rows = slice(c * blk, (c + 1) * blk)
        qf = q_ref[rows, :].astype(F32) * ((HEAD_DIM ** -0.5) * LOG2E)
        if c > MOBA_TOPK:
            g3 = lax.dot_general(kmean_parts, q_ref[rows, :], NT_DIMS, preferred_element_type=F32)
            gate_t = g3[0:n_blk] + g3[n_blk:2 * n_blk] + g3[2 * n_blk:3 * n_blk]
            rank = jnp.zeros((n_blk, blk), F32)
            for i in range(c):
                gi = gate_t[i:i + 1, :]
                tie_first = jnp.where(row > i, 1.0, 0.0)
                rank = rank + jnp.where(gi > gate_t, 1.0, jnp.where(gi == gate_t, tie_first, 0.0))
            neg_t = jnp.where(rank < float(MOBA_TOPK), 0.0, MASK_NEG)
            neg_t = jnp.where(row == c, 0.0, neg_t)
        else:
            neg_t = jnp.zeros((n_blk, blk), F32)
        aug_t = jnp.concatenate([neg_t, far_rows, pad_rows], axis=0).astype(BF16)
        aug_q = lax.dot_general(eye, aug_t, NT_DIMS, preferred_element_type=F32)
        qa_ref[rows, 0:HEAD_DIM] = qf.astype(BF16)
        qa_ref[rows, HEAD_DIM:2 * HEAD_DIM] = aug_q.astype(BF16)

    lane = lax.broadcasted_iota(jnp.int32, (blk, LANES), 1)
    far_ones = jnp.where((lane >= far_lanes[0]) & (lane < far_lanes[1]), 1.0, 0.0)

    def half_max(s):
        return jnp.maximum(s[:, :LANES], s[:, LANES:])

    group_ones = jnp.ones((ATTN_GROUP * blk, LANES), BF16)

    def q_tile(t, carry):
        qb0 = t * ATTN_Q_BLOCKS
        qrow = pl.multiple_of(t * q_rows, q_rows)
        n_groups = (qb0 + ATTN_Q_BLOCKS + ATTN_GROUP - 1) // ATTN_GROUP

        m128_ref[...] = jnp.full_like(m128_ref, MASK_NEG)

        def score_group(g, c):
            hm = None
            for u in range(ATTN_GROUP):
                j = g * ATTN_GROUP + u
                start = pl.multiple_of(j * blk, blk)
                k_aug = jnp.where(lane == j, 1.0, far_ones).astype(BF16)
                ka = jnp.concatenate([k_ref[pl.ds(start, blk), :], k_aug], axis=1)
                s = lax.dot_general(qa_ref[pl.ds(qrow, q_rows), :], ka, NT_DIMS,
                                    preferred_element_type=F32)
                s = jnp.concatenate(
                    [s[i * blk:(i + 1) * blk] + bias_ref[_bias_tile_index(qb0 + i, j)]
                     for i in range(ATTN_Q_BLOCKS)], axis=0)
                s_ref[j] = s
                hm = half_max(s) if hm is None else jnp.maximum(hm, half_max(s))
            m128_ref[...] = jnp.maximum(m128_ref[...], hm)
            return c

        lax.fori_loop(0, n_groups, score_group, 0)

        m = jnp.max(m128_ref[...], axis=1, keepdims=True)
        mb_ref[...] = jnp.broadcast_to(m, (q_rows, LANES))

        acc_ref[...] = jnp.zeros_like(acc_ref)

        def pv_group(g, c):
            mb = mb_ref[...]
            mb2 = jnp.concatenate([mb, mb], axis=1)
            p = jnp.concatenate([jnp.exp2(s_ref[g * ATTN_GROUP + u] - mb2).astype(BF16)
                                 for u in range(ATTN_GROUP)], axis=1)
            start = pl.multiple_of(g * (ATTN_GROUP * blk), ATTN_GROUP * blk)
            va = jnp.concatenate([v_ref[pl.ds(start, ATTN_GROUP * blk), :], group_ones], axis=1)
            acc_ref[...] += jnp.dot(p, va, preferred_element_type=F32)
            return c

        lax.fori_loop(0, n_groups, pv_group, 0)

        acc = acc_ref[...]
        o_ref[pl.ds(qrow, q_rows), :] = (acc[:, :HEAD_DIM] / acc[:, HEAD_DIM:]).astype(o_ref.dtype)
        return carry

    lax.fori_loop(0, n_blk // ATTN_Q_BLOCKS, q_tile, 0)


def _attention(proj3, bias_tiles, far_bias):
    bsz, seq, _ = proj3.shape
    n_blk = seq // MOBA_BLOCK
    blk = MOBA_BLOCK
    q_rows = ATTN_Q_BLOCKS * blk

    def head_cols(first):
        return pl.BlockSpec((None, seq, HEAD_DIM), lambda b, h: (b, 0, first + h))

    return pl.pallas_call(
        functools.partial(_attn_kernel, n_blk=n_blk),
        out_shape=jax.ShapeDtypeStruct((bsz, seq, ATTN_W), BF16),
        grid=(bsz, N_HEADS),
        in_specs=[
            pl.BlockSpec(memory_space=pltpu.SMEM),
            head_cols(0), head_cols(N_HEADS), head_cols(2 * N_HEADS),
            pl.BlockSpec((None, N_BIAS_TILES, blk, blk), lambda b, h: (h, 0, 0, 0)),
        ],
        out_specs=head_cols(0),
        scratch_shapes=[
            pltpu.VMEM((n_blk, HEAD_DIM), F32),
            pltpu.VMEM((seq, 2 * HEAD_DIM), BF16),
            pltpu.VMEM((n_blk, q_rows, blk), F32),
            pltpu.VMEM((q_rows, LANES), F32),
            pltpu.VMEM((q_rows, LANES), F32),
            pltpu.VMEM((q_rows, 2 * HEAD_DIM), F32),
        ],
        compiler_params=_params("parallel", "parallel"),
        name="moba_attention",
    )(far_bias, proj3, proj3, proj3, bias_tiles)


def _gelu_tanh(x):
    return 0.5 * x * (1.0 + jnp.tanh(math.sqrt(2.0 / math.pi) * (x + 0.044715 * (x * x * x))))


def _softplus(z):
    return jnp.maximum(z, 0.0) + jnp.log1p(jnp.exp(-jnp.abs(z)))


def _mix_kernel(xr_ref, gr_ref, scb_ref, scc_ref, scx_ref,
                cw_ref, cb_ref, wg_ref, ba_ref, bx_ref, lam_ref, sw_ref,
                yb_ref, yc_ref,
                xbuf, pbuf, a_s, u_s, hcar, *, ts):
    halo = SUBLANES
    first = pl.program_id(1) == 0

    @pl.when(first)
    def _():
        xbuf[0:halo, :] = jnp.zeros((halo, LRU_W), F32)
        pbuf[0:halo, :] = jnp.zeros((halo, SC_W), F32)
        hcar[...] = jnp.zeros_like(hcar)

    @pl.when(jnp.logical_not(first))
    def _():
        xbuf[0:halo, :] = xbuf[ts:ts + halo, :]
        pbuf[0:halo, :] = pbuf[ts:ts + halo, :]

    xbuf[halo:halo + ts, :] = xr_ref[...].astype(F32)
    xc = cb_ref[...] + cw_ref[LRU_CONV - 1:LRU_CONV, :] * xbuf[halo:halo + ts, :]
    for d in range(1, LRU_CONV):
        xc = xc + cw_ref[LRU_CONV - 1 - d:LRU_CONV - d, :] * xbuf[halo - d:halo - d + ts, :]
    xcb = xc.astype(BF16)
    neg_c_sp = -LRU_C * _softplus(-lam_ref[...])
    for g in range(LRU_BLOCKS):
        sl = slice(g * LRU_BW, (g + 1) * LRU_BW)
        ri = jnp.dot(xcb[:, sl], wg_ref[g], preferred_element_type=F32)
        r = jax.nn.sigmoid(ri[:, :LRU_BW] + ba_ref[:, sl])
        i = jax.nn.sigmoid(ri[:, LRU_BW:] + bx_ref[:, sl])
        log_a = neg_c_sp[:, sl] * r
        a = jnp.exp(log_a)
        mult = jnp.sqrt(-jnp.tanh(log_a) * (a * a + 1.0))
        a_s[:, sl] = a
        u_s[:, sl] = mult * i * xc[:, sl]

    rowi = lax.broadcasted_iota(jnp.int32, (SUBLANES, LRU_W), 0)

    def scan_body(t, hprev):
        r0 = pl.multiple_of(t * SUBLANES, SUBLANES)
        a = a_s[pl.ds(r0, SUBLANES), :]
        u = u_s[pl.ds(r0, SUBLANES), :]
        for sh in (1, 2, 4):
            a_sh = pltpu.roll(a, sh, axis=0)
            u_sh = pltpu.roll(u, sh, axis=0)
            ok = rowi >= sh
            u = jnp.where(ok, a * u_sh + u, u)
            a = jnp.where(ok, a * a_sh, a)
        hrow = a * hprev + u
        u_s[pl.ds(r0, SUBLANES), :] = hrow
        return jnp.broadcast_to(hrow[SUBLANES - 1:SUBLANES, :], (SUBLANES, LRU_W))

    hcar[...] = lax.fori_loop(0, ts // SUBLANES, scan_body, hcar[...], unroll=4)
    yb_ref[...] = (u_s[...] * _gelu_tanh(gr_ref[...].astype(F32))).astype(yb_ref.dtype)

    pbuf[halo:halo + ts, :] = scc_ref[...].astype(F32) * scx_ref[...].astype(F32)
    conv = sw_ref[SC_CONV - 1:SC_CONV, :] * pbuf[halo:halo + ts, :]
    for d in range(1, SC_CONV):
        conv = conv + sw_ref[SC_CONV - 1 - d:SC_CONV - d, :] * pbuf[halo - d:halo - d + ts, :]
    yc_ref[...] = (scb_ref[...].astype(F32) * conv).astype(yc_ref.dtype)


def _mixers(proj3, cw, cb3, wg_bf, ba3, bx3, lam3, sw, layer):
    bsz, seq, _ = proj3.shape
    ts = 512

    def col(c):
        return pl.BlockSpec((None, ts, LRU_W), lambda b, s: (b, s, c))

    def vec(width):
        return pl.BlockSpec((None, 1, width), lambda b, s: (layer, 0, 0))

    return pl.pallas_call(
        functools.partial(_mix_kernel, ts=ts),
        out_shape=(jax.ShapeDtypeStruct((bsz, seq, LRU_W), BF16),
                   jax.ShapeDtypeStruct((bsz, seq, SC_W), BF16)),
        grid=(bsz, seq // ts),
        in_specs=[
            col(COL_XR), col(COL_GR), col(COL_SCB), col(COL_SCC), col(COL_SCX),
            pl.BlockSpec((None, LRU_CONV, LRU_W), lambda b, s: (layer, 0, 0)),
            vec(LRU_W),
            pl.BlockSpec((None, LRU_BLOCKS, LRU_BW, 2 * LRU_BW), lambda b, s: (layer, 0, 0, 0)),
            vec(LRU_W), vec(LRU_W), vec(LRU_W),
            pl.BlockSpec((None, SC_CONV, SC_W), lambda b, s: (layer, 0, 0)),
        ],
        out_specs=(pl.BlockSpec((None, ts, LRU_W), lambda b, s: (b, s, 0)),
                   pl.BlockSpec((None, ts, SC_W), lambda b, s: (b, s, 0))),
        scratch_shapes=[
            pltpu.VMEM((SUBLANES + ts, LRU_W), F32),
            pltpu.VMEM((SUBLANES + ts, SC_W), F32),
            pltpu.VMEM((ts, LRU_W), F32),
            pltpu.VMEM((ts, LRU_W), F32),
            pltpu.VMEM((SUBLANES, LRU_W), F32),
        ],
        compiler_params=_params("parallel", "arbitrary"),
        name="lru_shortconv",
    )(proj3, proj3, proj3, proj3, proj3, cw, cb3, wg_bf, ba3, bx3, lam3, sw)


def _merge_kernel(ya_ref, yb_ref, yc_ref, ga_ref, gb_ref, gc_ref, ba_ref, bb_ref, bc_ref,
                  wa_ref, wb_ref, wc_ref, o_ref, wa_bf, wb_bf, wc_bf):
    def branch(y_ref, w_ref, wbf_ref, g_ref, b_ref):
        y = _stationary_weight_matmul(w_ref, wbf_ref, y_ref[...])
        return jax.nn.sigmoid(g_ref[...].astype(F32) + b_ref[...]) * y

    m = branch(ya_ref, wa_ref, wa_bf, ga_ref, ba_ref) + branch(yb_ref, wb_ref, wb_bf, gb_ref, bb_ref)
    m = m + branch(yc_ref, wc_ref, wc_bf, gc_ref, bc_ref)
    o_ref[...] = m.astype(o_ref.dtype)


def _merge(ya, yb, yc, proj, gbias3, w_a, w_b, w_c, layer):
    m_rows = ya.shape[0]
    tm, tn = 1024, 512
    n_tiles = D_MODEL // tn
    gate_tile0 = COL_GATES * 1024 // tn

    def rows(width):
        return pl.BlockSpec((tm, width), lambda n, m: (m, 0))

    def gate(k):
        return pl.BlockSpec((tm, tn), lambda n, m: (m, gate_tile0 + k * n_tiles + n))

    def gate_bias(k):
        return pl.BlockSpec((None, 1, tn), lambda n, m: (layer, 0, k * n_tiles + n))

    def weight(width):
        return pl.BlockSpec((None, width, tn), lambda n, m: (layer, 0, n))

    return pl.pallas_call(
        _merge_kernel,
        out_shape=jax.ShapeDtypeStruct((m_rows, D_MODEL), BF16),
        grid=(n_tiles, m_rows // tm),
        in_specs=[
            rows(ATTN_W), rows(LRU_W), rows(SC_W), gate(0), gate(1), gate(2),
            gate_bias(0), gate_bias(1), gate_bias(2),
            weight(ATTN_W), weight(LRU_W), weight(SC_W),
        ],
        out_specs=pl.BlockSpec((tm, tn), lambda n, m: (m, n)),
        scratch_shapes=[pltpu.VMEM((ATTN_W, tn), BF16), pltpu.VMEM((LRU_W, tn), BF16),
                        pltpu.VMEM((SC_W, tn), BF16)],
        compiler_params=_params("parallel", "arbitrary"),
        name="gated_merge",
    )(ya, yb, yc, proj, proj, proj, gbias3, gbias3, gbias3, w_a, w_b, w_c)


def _out_proj_kernel(m_ref, x_ref, w_ref, o_ref, wbf_ref):
    o_ref[...] = x_ref[...] + _stationary_weight_matmul(w_ref, wbf_ref, m_ref[...])


def _out_proj(merged, x2, w_out, layer):
    m_rows = x2.shape[0]
    tm, tn = 1024, 512
    return pl.pallas_call(
        _out_proj_kernel,
        out_shape=jax.ShapeDtypeStruct((m_rows, D_MODEL), F32),
        grid=(D_MODEL // tn, m_rows // tm),
        in_specs=[
            pl.BlockSpec((tm, D_MODEL), lambda n, m: (m, 0)),
            pl.BlockSpec((tm, tn), lambda n, m: (m, n)),
            pl.BlockSpec((None, D_MODEL, tn), lambda n, m: (layer, 0, n)),
        ],
        out_specs=pl.BlockSpec((tm, tn), lambda n, m: (m, n)),
        scratch_shapes=[pltpu.VMEM((D_MODEL, tn), BF16)],
        compiler_params=_params("parallel", "arbitrary"),
        name="out_proj",
    )(merged, x2, w_out)


def _mlp_kernel(x_ref, g_ref, wu_ref, wd_ref, ng_ref, *rest, last_layer):
    if last_layer:
        y_ref, h_ref, acc_ref = rest
    else:
        x_out_ref, h_next_ref, h_ref, acc_ref = rest
    f = pl.program_id(1)

    @pl.when(f == 0)
    def _():
        h_ref[...] = _rms(x_ref[...], g_ref[...]).astype(BF16)
        acc_ref[...] = jnp.zeros_like(acc_ref)

    up = jnp.dot(h_ref[...], wu_ref[...], preferred_element_type=F32)
    act = jnp.square(jnp.maximum(up, 0.0)).astype(BF16)
    acc_ref[...] += jnp.dot(act, wd_ref[...], preferred_element_type=F32)

    @pl.when(f == pl.num_programs(1) - 1)
    def _():
        x_new = x_ref[...] + acc_ref[...]
        normed = _rms(x_new, ng_ref[...])
        if last_layer:
            y_ref[...] = normed
        else:
            x_out_ref[...] = x_new
            h_next_ref[...] = normed.astype(h_next_ref.dtype)


def _mlp(x2, g3, wu_bf, wd_bf, next_g3, next_index, layer, last_layer):
    m_rows = x2.shape[0]
    tm, tf = 512, 1024
    row_spec = pl.BlockSpec((tm, D_MODEL), lambda m, f: (m, 0))
    x_shape = jax.ShapeDtypeStruct((m_rows, D_MODEL), F32)
    if last_layer:
        out_shape, out_specs = x_shape, row_spec
    else:
        out_shape = (x_shape, jax.ShapeDtypeStruct((m_rows, D_MODEL), BF16))
        out_specs = (row_spec, row_spec)
    return pl.pallas_call(
        functools.partial(_mlp_kernel, last_layer=last_layer),
        out_shape=out_shape,
        grid=(m_rows // tm, D_FF // tf),
        in_specs=[
            row_spec,
            pl.BlockSpec((None, 1, D_MODEL), lambda m, f: (layer, 0, 0)),
            pl.BlockSpec((None, D_MODEL, tf), lambda m, f: (layer, 0, f)),
            pl.BlockSpec((None, tf, D_MODEL), lambda m, f: (layer, f, 0)),
            pl.BlockSpec((None, 1, D_MODEL), lambda m, f: (next_index, 0, 0)),
        ],
        out_specs=out_specs,
        scratch_shapes=[pltpu.VMEM((tm, D_MODEL), BF16), pltpu.VMEM((tm, D_MODEL), F32)],
        compiler_params=_params("parallel", "arbitrary"),
        name="mlp",
    )(x2, g3, wu_bf, wd_bf, next_g3)


def kernel(x, norm_mix_g, w_in, gate_b, rel_table, w_attn_o, lru_conv_w, lru_conv_b,
           lru_wa, lru_ba, lru_wx, lru_bx, lru_lambda, w_lru_o, sc_conv_w, w_sc_o,
           w_out, norm_mlp_g, w_mlp_up, w_mlp_down, final_g):
    bsz, seq, d_model = x.shape
    assert d_model == D_MODEL and seq % MOBA_BLOCK == 0
    assert w_in.shape == (DEPTH, D_MODEL, N_IN)
    m_rows = bsz * seq

    def vec3(v):
        return v.reshape(v.shape[0], 1, v.shape[1])

    w_up_bf = w_mlp_up.astype(BF16)
    w_down_bf = w_mlp_down.astype(BF16)
    w_gates_bf = jnp.concatenate([lru_wa, lru_wx], axis=-1).astype(BF16)

    bias_tiles = _bias_tiles(rel_table)
    far_bias = rel_table[N_BUCKETS - 1]
    mix_g3 = vec3(norm_mix_g)
    final_g3 = final_g.reshape(1, 1, D_MODEL)

    x2 = x.reshape(m_rows, D_MODEL)
    h = _norm(x2, mix_g3, 0)
    for layer in range(DEPTH):
        proj = _in_proj(h, w_in, layer)
        proj3 = proj.reshape(bsz, seq, N_IN)
        ya = _attention(proj3, bias_tiles, far_bias)
        yb, yc = _mixers(proj3, lru_conv_w, vec3(lru_conv_b), w_gates_bf, vec3(lru_ba),
                         vec3(lru_bx), vec3(lru_lambda), sc_conv_w, layer)
        merged = _merge(ya.reshape(m_rows, ATTN_W), yb.reshape(m_rows, LRU_W),
                        yc.reshape(m_rows, SC_W), proj, vec3(gate_b),
                        w_attn_o, w_lru_o, w_sc_o, layer)
        x2 = _out_proj(merged, x2, w_out, layer)
        if layer < DEPTH - 1:
            x2, h = _mlp(x2, vec3(norm_mlp_g), w_up_bf, w_down_bf, mix_g3, layer + 1, layer, False)
        else:
            x2 = _mlp(x2, vec3(norm_mlp_g), w_up_bf, w_down_bf, final_g3, 0, layer, True)
    return x2.reshape(bsz, seq, D_MODEL)
```

```python
import functools
import math

import jax
import jax.numpy as jnp
from jax import lax
from jax.experimental import pallas as pl
from jax.experimental.pallas import tpu as pltpu

D_MODEL = 2048
DEPTH = 4
N_HEADS = 8
HEAD_DIM = 128
ATTN_W = N_HEADS * HEAD_DIM
MOBA_BLOCK = 256
MOBA_TOPK = 3
N_BUCKETS = 32
MAX_DISTANCE = 128
LRU_W = D_MODEL // 2
LRU_BLOCKS = 8
LRU_BW = LRU_W // LRU_BLOCKS
LRU_CONV = 4
LRU_C = 8.0
SC_W = D_MODEL // 2
SC_CONV = 3
D_FF = 4 * D_MODEL
N_BRANCH = 3
EPS = 1e-6
N_IN = 3 * ATTN_W + 2 * LRU_W + 3 * SC_W + N_BRANCH * D_MODEL

COL_XR, COL_GR, COL_SCB, COL_SCC, COL_SCX = 3, 4, 5, 6, 7
COL_GATES = 8

LANES = 128
SUBLANES = 8
MASK_NEG = -1e30
LOG2E = math.log2(math.e)
N_FAR_PARTS = 3
VMEM_LIMIT = 56 * 1024 * 1024

F32 = jnp.float32
BF16 = jnp.bfloat16
NT_DIMS = (((1,), (1,)), ((), ()))


def _params(*sem):
    return pltpu.CompilerParams(dimension_semantics=sem, vmem_limit_bytes=VMEM_LIMIT)


def _rms(x, g):
    ms = jnp.mean(x * x, axis=-1, keepdims=True)
    return x * lax.rsqrt(ms + EPS) * g


N_BIAS_TILES = 4


def _bias_tile_index(q_blk, k_blk):
    return jnp.clip(q_blk - k_blk, -1, N_BIAS_TILES - 2) + 1


def _bias_kernel(tbl_ref, o_ref):
    h = pl.program_id(0)
    far = tbl_ref[N_BUCKETS - 1, h]
    r = lax.broadcasted_iota(jnp.int32, (MOBA_BLOCK, MOBA_BLOCK), 0)
    c = lax.broadcasted_iota(jnp.int32, (MOBA_BLOCK, MOBA_BLOCK), 1)
    max_exact = N_BUCKETS // 2
    o_ref[0] = jnp.full((MOBA_BLOCK, MOBA_BLOCK), MASK_NEG, F32)
    o_ref[3] = jnp.zeros((MOBA_BLOCK, MOBA_BLOCK), F32)
    for t in range(2):
        dist = r - c + t * MOBA_BLOCK
        n = jnp.maximum(dist, 0)
        nf = jnp.maximum(n, 1).astype(F32)
        large = max_exact + (jnp.log(nf / max_exact) / math.log(MAX_DISTANCE / max_exact)
                             * (N_BUCKETS - max_exact)).astype(jnp.int32)
        large = jnp.minimum(large, N_BUCKETS - 1)
        bkt = jnp.where(n < max_exact, n, large)
        b = jnp.zeros((MOBA_BLOCK, MOBA_BLOCK), F32)
        for k in range(N_BUCKETS):
            b = jnp.where(bkt == k, tbl_ref[k, h], b)
        b = (b - far) * LOG2E
        if t == 0:
            b = jnp.where(dist >= 0, b, MASK_NEG)
        o_ref[1 + t] = b


def _bias_tiles(rel_table):
    tile_shape = (N_BIAS_TILES, MOBA_BLOCK, MOBA_BLOCK)
    return pl.pallas_call(
        _bias_kernel,
        out_shape=jax.ShapeDtypeStruct((N_HEADS,) + tile_shape, F32),
        grid=(N_HEADS,),
        in_specs=[pl.BlockSpec(memory_space=pltpu.SMEM)],
        out_specs=pl.BlockSpec((None,) + tile_shape, lambda h: (h, 0, 0, 0)),
        compiler_params=_params("arbitrary"),
        name="t5_bias_tiles",
    )(rel_table)


def _norm_kernel(x_ref, g_ref, o_ref):
    o_ref[...] = _rms(x_ref[...], g_ref[...]).astype(o_ref.dtype)


def _norm(x2, g3, layer):
    m_rows = x2.shape[0]
    tm = 512
    return pl.pallas_call(
        _norm_kernel,
        out_shape=jax.ShapeDtypeStruct((m_rows, D_MODEL), BF16),
        grid=(m_rows // tm,),
        in_specs=[pl.BlockSpec((tm, D_MODEL), lambda m: (m, 0)),
                  pl.BlockSpec((None, 1, D_MODEL), lambda m: (layer, 0, 0))],
        out_specs=pl.BlockSpec((tm, D_MODEL), lambda m: (m, 0)),
        compiler_params=_params("parallel"),
        name="rms_norm",
    )(x2, g3)


def _stationary_weight_matmul(w_ref, wbf_ref, lhs):
    @pl.when(pl.program_id(1) == 0)
    def _():
        wbf_ref[...] = w_ref[...].astype(BF16)

    return jnp.dot(lhs, wbf_ref[...], preferred_element_type=F32)


def _ride_along_cast_specs(weights, layer, n_steps, step_index):
    in_specs, out_specs, out_shapes = [], [], []
    for w in weights:
        rows, cols = w.shape[1] // n_steps, w.shape[2]
        assert rows * n_steps == w.shape[1] and rows % (2 * SUBLANES) == 0
        in_specs.append(pl.BlockSpec((None, rows, cols), lambda *g: (layer, step_index(*g), 0)))
        out_specs.append(pl.BlockSpec((rows, cols), lambda *g: (step_index(*g), 0)))
        out_shapes.append(jax.ShapeDtypeStruct(w.shape[1:], BF16))
    return in_specs, out_specs, out_shapes


def _ride_along_cast(src_refs, dst_refs):
    for src, dst in zip(src_refs, dst_refs):
        dst[...] = src[...].astype(dst.dtype)


def _in_proj_kernel(h_ref, w_ref, o_ref, wbf_ref):
    o_ref[...] = _stationary_weight_matmul(w_ref, wbf_ref, h_ref[...]).astype(o_ref.dtype)


def _in_proj(h, w_in, layer):
    m_rows = h.shape[0]
    tm, tn = 1024, 1024
    return pl.pallas_call(
        _in_proj_kernel,
        out_shape=jax.ShapeDtypeStruct((m_rows, N_IN), BF16),
        grid=(N_IN // tn, m_rows // tm),
        in_specs=[
            pl.BlockSpec((tm, D_MODEL), lambda n, m: (m, 0)),
            pl.BlockSpec((None, D_MODEL, tn), lambda n, m: (layer, 0, n)),
        ],
        out_specs=pl.BlockSpec((tm, tn), lambda n, m: (m, n)),
        scratch_shapes=[pltpu.VMEM((D_MODEL, tn), BF16)],
        compiler_params=_params("parallel", "arbitrary"),
        name="in_proj",
    )(h, w_in)


ATTN_GROUP = 4
ATTN_Q_BLOCKS = 4


def _split_bf16(x, parts):
    pieces = []
    for _ in range(parts):
        piece = x.astype(BF16).astype(F32)
        pieces.append(piece)
        x = x - piece
    return pieces


def _attn_kernel(far_ref, q_ref, k_ref, v_ref, bias_ref, *rest, n_blk, n_cast):
    cast_in, (o_ref,), cast_out = rest[:n_cast], rest[n_cast:n_cast + 1], rest[n_cast + 1:2 * n_cast + 1]
    kmean_ref, qa_ref, s_ref, m128_ref, mb_ref, acc_ref = rest[2 * n_cast + 1:]
    _ride_along_cast(cast_in, cast_out)
    h = pl.program_id(1)
    blk = MOBA_BLOCK
    q_rows = ATTN_Q_BLOCKS * blk
    far_lanes = (n_blk, n_blk + N_FAR_PARTS)
    assert n_blk % SUBLANES == 0 and N_FAR_PARTS <= SUBLANES and n_blk + SUBLANES <= LANES
    assert n_blk % ATTN_GROUP == 0 and n_blk % ATTN_Q_BLOCKS == 0

    for j in range(n_blk):
        kj = k_ref[j * blk:(j + 1) * blk, :].astype(F32)
        kmean_ref[j:j + 1, :] = jnp.sum(kj, axis=0, keepdims=True) * (1.0 / blk)
    kmean_parts = jnp.concatenate(_split_bf16(kmean_ref[...], 3), axis=0).astype(BF16)

    far = jnp.full((SUBLANES, blk), far_ref[h], F32) * LOG2E
    row8 = lax.broadcasted_iota(jnp.int32, (SUBLANES, blk), 0)
    far_rows = jnp.zeros((SUBLANES, blk), F32)
    for i, piece in enumerate(_split_bf16(far, N_FAR_PARTS)):
        far_rows = jnp.where(row8 == i, piece, far_rows)

    eye = jnp.where(lax.broadcasted_iota(jnp.int32, (blk, blk), 0)
                    == lax.broadcasted_iota(jnp.int32, (blk, blk), 1), 1.0, 0.0).astype(BF16)
    row = lax.broadcasted_iota(jnp.int32, (n_blk, blk), 0)
    pad_rows = jnp.zeros((LANES - n_blk - SUBLANES, blk), F32)

    for c in range(n_blk):
        rows = slice(c * blk, (c + 1) * blk)
        qf = q_ref[rows, :].astype(F32) * ((HEAD_DIM ** -0.5) * LOG2E)
        if c > MOBA_TOPK:
            g3 = lax.dot_general(kmean_parts, q_ref[rows, :], NT_DIMS, preferred_element_type=F32)
            gate_t = g3[0:n_blk] + g3[n_blk:2 * n_blk] + g3[2 * n_blk:3 * n_blk]
            rank = jnp.zeros((n_blk, blk), F32)
            for i in range(c):
                gi = gate_t[i:i + 1, :]
                tie_first = jnp.where(row > i, 1.0, 0.0)
                rank = rank + jnp.where(gi > gate_t, 1.0, jnp.where(gi == gate_t, tie_first, 0.0))
            neg_t = jnp.where(rank < float(MOBA_TOPK), 0.0, MASK_NEG)
            neg_t = jnp.where(row == c, 0.0, neg_t)
        else:
            neg_t = jnp.zeros((n_blk, blk), F32)
        aug_t = jnp.concatenate([neg_t, far_rows, pad_rows], axis=0).astype(BF16)
        aug_q = lax.dot_general(eye, aug_t, NT_DIMS, preferred_element_type=F32)
        qa_ref[rows, 0:HEAD_DIM] = qf.astype(BF16)
        qa_ref[rows, HEAD_DIM:2 * HEAD_DIM] = aug_q.astype(BF16)

    lane = lax.broadcasted_iota(jnp.int32, (blk, LANES), 1)
    far_ones = jnp.where((lane >= far_lanes[0]) & (lane < far_lanes[1]), 1.0, 0.0)

    def half_max(s):
        return jnp.maximum(s[:, :LANES], s[:, LANES:])

    group_ones = jnp.ones((ATTN_GROUP * blk, LANES), BF16)

    def q_tile(t, carry):
        qb0 = t * ATTN_Q_BLOCKS
        qrow = pl.multiple_of(t * q_rows, q_rows)
        n_groups = (qb0 + ATTN_Q_BLOCKS + ATTN_GROUP - 1) // ATTN_GROUP

        m128_ref[...] = jnp.full_like(m128_ref, MASK_NEG)

        def score_group(g, c):
            hm = None
            for u in range(ATTN_GROUP):
                j = g * ATTN_GROUP + u
                start = pl.multiple_of(j * blk, blk)
                k_aug = jnp.where(lane == j, 1.0, far_ones).astype(BF16)
                ka = jnp.concatenate([k_ref[pl.ds(start, blk), :], k_aug], axis=1)
                s = lax.dot_general(qa_ref[pl.ds(qrow, q_rows), :], ka, NT_DIMS,
                                    preferred_element_type=F32)
                s = jnp.concatenate(
                    [s[i * blk:(i + 1) * blk] + bias_ref[_bias_tile_index(qb0 + i, j)]
                     for i in range(ATTN_Q_BLOCKS)], axis=0)
                s_ref[j] = s
                hm = half_max(s) if hm is None else jnp.maximum(hm, half_max(s))
            m128_ref[...] = jnp.maximum(m128_ref[...], hm)
            return c

        lax.fori_loop(0, n_groups, score_group, 0)

        m = jnp.max(m128_ref[...], axis=1, keepdims=True)
        mb_ref[...] = jnp.broadcast_to(m, (q_rows, LANES))

        acc_ref[...] = jnp.zeros_like(acc_ref)

        def pv_group(g, c):
            mb = mb_ref[...]
            mb2 = jnp.concatenate([mb, mb], axis=1)
            p = jnp.concatenate([jnp.exp2(s_ref[g * ATTN_GROUP + u] - mb2).astype(BF16)
                                 for u in range(ATTN_GROUP)], axis=1)
            start = pl.multiple_of(g * (ATTN_GROUP * blk), ATTN_GROUP * blk)
            va = jnp.concatenate([v_ref[pl.ds(start, ATTN_GROUP * blk), :], group_ones], axis=1)
            acc_ref[...] += jnp.dot(p, va, preferred_element_type=F32)
            return c

        lax.fori_loop(0, n_groups, pv_group, 0)

        acc = acc_ref[...]
        o_ref[pl.ds(qrow, q_rows), :] = (acc[:, :HEAD_DIM] / acc[:, HEAD_DIM:]).astype(o_ref.dtype)
        return carry

    lax.fori_loop(0, n_blk // ATTN_Q_BLOCKS, q_tile, 0)


def _attention(proj3, bias_tiles, far_bias, cast_weights, layer):
    bsz, seq, _ = proj3.shape
    n_blk = seq // MOBA_BLOCK
    blk = MOBA_BLOCK
    q_rows = ATTN_Q_BLOCKS * blk

    def head_cols(first):
        return pl.BlockSpec((None, seq, HEAD_DIM), lambda b, h: (b, 0, first + h))

    cast_in, cast_out, cast_shapes = _ride_along_cast_specs(
        cast_weights, layer, bsz * N_HEADS, lambda b, h: b * N_HEADS + h)
    return pl.pallas_call(
        functools.partial(_attn_kernel, n_blk=n_blk, n_cast=len(cast_weights)),
        out_shape=[jax.ShapeDtypeStruct((bsz, seq, ATTN_W), BF16)] + cast_shapes,
        grid=(bsz, N_HEADS),
        in_specs=[
            pl.BlockSpec(memory_space=pltpu.SMEM),
            head_cols(0), head_cols(N_HEADS), head_cols(2 * N_HEADS),
            pl.BlockSpec((None, N_BIAS_TILES, blk, blk), lambda b, h: (h, 0, 0, 0)),
        ] + cast_in,
        out_specs=[head_cols(0)] + cast_out,
        scratch_shapes=[
            pltpu.VMEM((n_blk, HEAD_DIM), F32),
            pltpu.VMEM((seq, 2 * HEAD_DIM), BF16),
            pltpu.VMEM((n_blk, q_rows, blk), F32),
            pltpu.VMEM((q_rows, LANES), F32),
            pltpu.VMEM((q_rows, LANES), F32),
            pltpu.VMEM((q_rows, 2 * HEAD_DIM), F32),
        ],
        compiler_params=_params("parallel", "parallel"),
        name="moba_attention",
    )(far_bias, proj3, proj3, proj3, bias_tiles, *cast_weights)


def _gelu_tanh(x):
    return 0.5 * x * (1.0 + jnp.tanh(math.sqrt(2.0 / math.pi) * (x + 0.044715 * (x * x * x))))


def _softplus(z):
    return jnp.maximum(z, 0.0) + jnp.log1p(jnp.exp(-jnp.abs(z)))


def _mix_kernel(xr_ref, gr_ref, scb_ref, scc_ref, scx_ref,
                cw_ref, cb_ref, wg_ref, ba_ref, bx_ref, lam_ref, sw_ref,
                *rest, ts, n_cast):
    cast_in, (yb_ref, yc_ref), cast_out = rest[:n_cast], rest[n_cast:n_cast + 2], rest[n_cast + 2:2 * n_cast + 2]
    xbuf, pbuf, a_s, u_s, hcar = rest[2 * n_cast + 2:]
    _ride_along_cast(cast_in, cast_out)
    halo = SUBLANES
    first = pl.program_id(1) == 0

    @pl.when(first)
    def _():
        xbuf[0:halo, :] = jnp.zeros((halo, LRU_W), F32)
        pbuf[0:halo, :] = jnp.zeros((halo, SC_W), F32)
        hcar[...] = jnp.zeros_like(hcar)

    @pl.when(jnp.logical_not(first))
    def _():
        xbuf[0:halo, :] = xbuf[ts:ts + halo, :]
        pbuf[0:halo, :] = pbuf[ts:ts + halo, :]

    xbuf[halo:halo + ts, :] = xr_ref[...].astype(F32)
    xc = cb_ref[...] + cw_ref[LRU_CONV - 1:LRU_CONV, :] * xbuf[halo:halo + ts, :]
    for d in range(1, LRU_CONV):
        xc = xc + cw_ref[LRU_CONV - 1 - d:LRU_CONV - d, :] * xbuf[halo - d:halo - d + ts, :]
    xcb = xc.astype(BF16)
    neg_c_sp = -LRU_C * _softplus(-lam_ref[...])
    for g in range(LRU_BLOCKS):
        sl = slice(g * LRU_BW, (g + 1) * LRU_BW)
        ri = jnp.dot(xcb[:, sl], wg_ref[g], preferred_element_type=F32)
        r = jax.nn.sigmoid(ri[:, :LRU_BW] + ba_ref[:, sl])
        i = jax.nn.sigmoid(ri[:, LRU_BW:] + bx_ref[:, sl])
        log_a = neg_c_sp[:, sl] * r
        a = jnp.exp(log_a)
        mult = jnp.sqrt(-jnp.tanh(log_a) * (a * a + 1.0))
        a_s[:, sl] = a
        u_s[:, sl] = mult * i * xc[:, sl]

    rowi = lax.broadcasted_iota(jnp.int32, (SUBLANES, LRU_W), 0)

    def scan_body(t, hprev):
        r0 = pl.multiple_of(t * SUBLANES, SUBLANES)
        a = a_s[pl.ds(r0, SUBLANES), :]
        u = u_s[pl.ds(r0, SUBLANES), :]
        for sh in (1, 2, 4):
            a_sh = pltpu.roll(a, sh, axis=0)
            u_sh = pltpu.roll(u, sh, axis=0)
            ok = rowi >= sh
            u = jnp.where(ok, a * u_sh + u, u)
            a = jnp.where(ok, a * a_sh, a)
        hrow = a * hprev + u
        u_s[pl.ds(r0, SUBLANES), :] = hrow
        return jnp.broadcast_to(hrow[SUBLANES - 1:SUBLANES, :], (SUBLANES, LRU_W))

    hcar[...] = lax.fori_loop(0, ts // SUBLANES, scan_body, hcar[...], unroll=4)
    yb_ref[...] = (u_s[...] * _gelu_tanh(gr_ref[...].astype(F32))).astype(yb_ref.dtype)

    pbuf[halo:halo + ts, :] = scc_ref[...].astype(F32) * scx_ref[...].astype(F32)
    conv = sw_ref[SC_CONV - 1:SC_CONV, :] * pbuf[halo:halo + ts, :]
    for d in range(1, SC_CONV):
        conv = conv + sw_ref[SC_CONV - 1 - d:SC_CONV - d, :] * pbuf[halo - d:halo - d + ts, :]
    yc_ref[...] = (scb_ref[...].astype(F32) * conv).astype(yc_ref.dtype)


def _mixers(proj3, cw, cb3, wg_bf, ba3, bx3, lam3, sw, cast_weights, layer):
    bsz, seq, _ = proj3.shape
    ts = 512
    n_seq = seq // ts

    def col(c):
        return pl.BlockSpec((None, ts, LRU_W), lambda b, s: (b, s, c))

    def vec(width):
        return pl.BlockSpec((None, 1, width), lambda b, s: (layer, 0, 0))

    cast_in, cast_out, cast_shapes = _ride_along_cast_specs(
        cast_weights, layer, bsz * n_seq, lambda b, s: b * n_seq + s)
    return pl.pallas_call(
        functools.partial(_mix_kernel, ts=ts, n_cast=len(cast_weights)),
        out_shape=[jax.ShapeDtypeStruct((bsz, seq, LRU_W), BF16),
                   jax.ShapeDtypeStruct((bsz, seq, SC_W), BF16)] + cast_shapes,
        grid=(bsz, n_seq),
        in_specs=[
            col(COL_XR), col(COL_GR), col(COL_SCB), col(COL_SCC), col(COL_SCX),
            pl.BlockSpec((None, LRU_CONV, LRU_W), lambda b, s: (layer, 0, 0)),
            vec(LRU_W),
            pl.BlockSpec((None, LRU_BLOCKS, LRU_BW, 2 * LRU_BW), lambda b, s: (layer, 0, 0, 0)),
            vec(LRU_W), vec(LRU_W), vec(LRU_W),
            pl.BlockSpec((None, SC_CONV, SC_W), lambda b, s: (layer, 0, 0)),
        ] + cast_in,
        out_specs=[pl.BlockSpec((None, ts, LRU_W), lambda b, s: (b, s, 0)),
                   pl.BlockSpec((None, ts, SC_W), lambda b, s: (b, s, 0))] + cast_out,
        scratch_shapes=[
            pltpu.VMEM((SUBLANES + ts, LRU_W), F32),
            pltpu.VMEM((SUBLANES + ts, SC_W), F32),
            pltpu.VMEM((ts, LRU_W), F32),
            pltpu.VMEM((ts, LRU_W), F32),
            pltpu.VMEM((SUBLANES, LRU_W), F32),
        ],
        compiler_params=_params("parallel", "arbitrary"),
        name="lru_shortconv",
    )(proj3, proj3, proj3, proj3, proj3, cw, cb3, wg_bf, ba3, bx3, lam3, sw, *cast_weights)


def _merge_out_kernel(ya_ref, yb_ref, yc_ref, ga_ref, gb_ref, gc_ref, gbias_ref, x_ref,
                      wa_ref, wb_ref, wc_ref, wo_ref, o_ref):
    def branch(y_ref, w_ref, g_ref, k):
        y = jnp.dot(y_ref[...], w_ref[...], preferred_element_type=F32)
        gate = jax.nn.sigmoid(g_ref[...].astype(F32) + gbias_ref[:, k * D_MODEL:(k + 1) * D_MODEL])
        return gate * y

    m = branch(ya_ref, wa_ref, ga_ref, 0) + branch(yb_ref, wb_ref, gb_ref, 1)
    m = m + branch(yc_ref, wc_ref, gc_ref, 2)
    o_ref[...] = x_ref[...] + jnp.dot(m.astype(BF16), wo_ref[...], preferred_element_type=F32)


def _merge_out(ya, yb, yc, proj, gbias3, x2, wa_bf, wb_bf, wc_bf, wo_bf, layer):
    m_rows = ya.shape[0]
    tm = 256
    gate_tile = COL_GATES * 1024 // D_MODEL

    def rows(width):
        return pl.BlockSpec((tm, width), lambda m: (m, 0))

    def gate(k):
        return pl.BlockSpec((tm, D_MODEL), lambda m: (m, gate_tile + k))

    def resident(w):
        return pl.BlockSpec(w.shape, lambda m: (0, 0), pipeline_mode=pl.Buffered(1))

    return pl.pallas_call(
        _merge_out_kernel,
        out_shape=jax.ShapeDtypeStruct((m_rows, D_MODEL), F32),
        grid=(m_rows // tm,),
        in_specs=[
            rows(ATTN_W), rows(LRU_W), rows(SC_W), gate(0), gate(1), gate(2),
            pl.BlockSpec((None, 1, N_BRANCH * D_MODEL), lambda m: (layer, 0, 0)),
            rows(D_MODEL),
            resident(wa_bf), resident(wb_bf), resident(wc_bf), resident(wo_bf),
        ],
        out_specs=rows(D_MODEL),
        compiler_params=_params("parallel"),
        name="merge_out_proj",
    )(ya, yb, yc, proj, proj, proj, gbias3, x2, wa_bf, wb_bf, wc_bf, wo_bf)


def _mlp_kernel(x_ref, g_ref, wu_ref, wd_ref, ng_ref, *rest, last_layer):
    if last_layer:
        y_ref, h_ref, acc_ref = rest
    else:
        x_out_ref, h_next_ref, h_ref, acc_ref = rest
    f = pl.program_id(1)

    @pl.when(f == 0)
    def _():
        h_ref[...] = _rms(x_ref[...], g_ref[...]).astype(BF16)
        acc_ref[...] = jnp.zeros_like(acc_ref)

    up = jnp.dot(h_ref[...], wu_ref[...], preferred_element_type=F32)
    act = jnp.square(jnp.maximum(up, 0.0)).astype(BF16)
    acc_ref[...] += jnp.dot(act, wd_ref[...], preferred_element_type=F32)

    @pl.when(f == pl.num_programs(1) - 1)
    def _():
        x_new = x_ref[...] + acc_ref[...]
        normed = _rms(x_new, ng_ref[...])
        if last_layer:
            y_ref[...] = normed
        else:
            x_out_ref[...] = x_new
            h_next_ref[...] = normed.astype(h_next_ref.dtype)


def _mlp(x2, g3, wu_bf, wd_bf, next_g3, next_index, layer, last_layer):
    m_rows = x2.shape[0]
    tm, tf = 512, 1024
    row_spec = pl.BlockSpec((tm, D_MODEL), lambda m, f: (m, 0))
    x_shape = jax.ShapeDtypeStruct((m_rows, D_MODEL), F32)
    if last_layer:
        out_shape, out_specs = x_shape, row_spec
    else:
        out_shape = (x_shape, jax.ShapeDtypeStruct((m_rows, D_MODEL), BF16))
        out_specs = (row_spec, row_spec)
    return pl.pallas_call(
        functools.partial(_mlp_kernel, last_layer=last_layer),
        out_shape=out_shape,
        grid=(m_rows // tm, D_FF // tf),
        in_specs=[
            row_spec,
            pl.BlockSpec((None, 1, D_MODEL), lambda m, f: (layer, 0, 0)),
            pl.BlockSpec((D_MODEL, tf), lambda m, f: (0, f)),
            pl.BlockSpec((tf, D_MODEL), lambda m, f: (f, 0)),
            pl.BlockSpec((None, 1, D_MODEL), lambda m, f: (next_index, 0, 0)),
        ],
        out_specs=out_specs,
        scratch_shapes=[pltpu.VMEM((tm, D_MODEL), BF16), pltpu.VMEM((tm, D_MODEL), F32)],
        compiler_params=_params("parallel", "arbitrary"),
        name="mlp",
    )(x2, g3, wu_bf, wd_bf, next_g3)


def kernel(x, norm_mix_g, w_in, gate_b, rel_table, w_attn_o, lru_conv_w, lru_conv_b,
           lru_wa, lru_ba, lru_wx, lru_bx, lru_lambda, w_lru_o, sc_conv_w, w_sc_o,
           w_out, norm_mlp_g, w_mlp_up, w_mlp_down, final_g):
    bsz, seq, d_model = x.shape
    assert d_model == D_MODEL and seq % MOBA_BLOCK == 0
    assert w_in.shape == (DEPTH, D_MODEL, N_IN)
    m_rows = bsz * seq

    def vec3(v):
        return v.reshape(v.shape[0], 1, v.shape[1])

    w_gates_bf = jnp.concatenate([lru_wa, lru_wx], axis=-1).astype(BF16)

    bias_tiles = _bias_tiles(rel_table)
    far_bias = rel_table[N_BUCKETS - 1]
    mix_g3 = vec3(norm_mix_g)
    mlp_g3 = vec3(norm_mlp_g)
    final_g3 = final_g.reshape(1, 1, D_MODEL)

    x2 = x.reshape(m_rows, D_MODEL)
    h = _norm(x2, mix_g3, 0)
    for layer in range(DEPTH):
        proj = _in_proj(h, w_in, layer)
        proj3 = proj.reshape(bsz, seq, N_IN)
        ya, wa_bf, wb_bf, wc_bf, wo_bf = _attention(
            proj3, bias_tiles, far_bias, (w_attn_o, w_lru_o, w_sc_o, w_out), layer)
        yb, yc, wu_bf, wd_bf = _mixers(
            proj3, lru_conv_w, vec3(lru_conv_b), w_gates_bf, vec3(lru_ba), vec3(lru_bx),
            vec3(lru_lambda), sc_conv_w, (w_mlp_up, w_mlp_down), layer)
        x2 = _merge_out(ya.reshape(m_rows, ATTN_W), yb.reshape(m_rows, LRU_W),
                        yc.reshape(m_rows, SC_W), proj, vec3(gate_b), x2,
                        wa_bf, wb_bf, wc_bf, wo_bf, layer)
        if layer < DEPTH - 1:
            x2, h = _mlp(x2, mlp_g3, wu_bf, wd_bf, mix_g3, layer + 1, layer, False)
        else:
            x2 = _mlp(x2, mlp_g3, wu_bf, wd_bf, final_g3, 0, layer, True)
    return x2.reshape(bsz, seq, D_MODEL)
```

```python
import functools
import math

import jax
import jax.numpy as jnp
from jax import lax
from jax.experimental import pallas as pl
from jax.experimental.pallas import tpu as pltpu

D_MODEL = 2048
DEPTH = 4
N_HEADS = 8
HEAD_DIM = 128
ATTN_W = N_HEADS * HEAD_DIM
MOBA_BLOCK = 256
MOBA_TOPK = 3
N_BUCKETS = 32
MAX_DISTANCE = 128
LRU_W = D_MODEL // 2
LRU_BLOCKS = 8
LRU_BW = LRU_W // LRU_BLOCKS
LRU_CONV = 4
LRU_C = 8.0
SC_W = D_MODEL // 2
SC_CONV = 3
D_FF = 4 * D_MODEL
N_BRANCH = 3
EPS = 1e-6
N_IN = 3 * ATTN_W + 2 * LRU_W + 3 * SC_W + N_BRANCH * D_MODEL

COL_XR, COL_GR, COL_SCB, COL_SCC, COL_SCX = 3, 4, 5, 6, 7
COL_GATES = 8

LANES = 128
SUBLANES = 8
MASK_NEG = -1e30
LOG2E = math.log2(math.e)
N_FAR_PARTS = 3
VMEM_LIMIT = 56 * 1024 * 1024

F32 = jnp.float32
BF16 = jnp.bfloat16
NT_DIMS = (((1,), (1,)), ((), ()))


def _params(*sem):
    return pltpu.CompilerParams(dimension_semantics=sem, vmem_limit_bytes=VMEM_LIMIT)


def _rms(x, g):
    ms = jnp.mean(x * x, axis=-1, keepdims=True)
    return x * lax.rsqrt(ms + EPS) * g


N_BIAS_TILES = 4


def _bias_tile_index(q_blk, k_blk):
    return jnp.clip(q_blk - k_blk, -1, N_BIAS_TILES - 2) + 1


def _bias_kernel(tbl_ref, o_ref):
    h = pl.program_id(0)
    far = tbl_ref[N_BUCKETS - 1, h]
    r = lax.broadcasted_iota(jnp.int32, (MOBA_BLOCK, MOBA_BLOCK), 0)
    c = lax.broadcasted_iota(jnp.int32, (MOBA_BLOCK, MOBA_BLOCK), 1)
    max_exact = N_BUCKETS // 2
    o_ref[0] = jnp.full((MOBA_BLOCK, MOBA_BLOCK), MASK_NEG, F32)
    o_ref[3] = jnp.zeros((MOBA_BLOCK, MOBA_BLOCK), F32)
    for t in range(2):
        dist = r - c + t * MOBA_BLOCK
        n = jnp.maximum(dist, 0)
        nf = jnp.maximum(n, 1).astype(F32)
        large = max_exact + (jnp.log(nf / max_exact) / math.log(MAX_DISTANCE / max_exact)
                             * (N_BUCKETS - max_exact)).astype(jnp.int32)
        large = jnp.minimum(large, N_BUCKETS - 1)
        bkt = jnp.where(n < max_exact, n, large)
        b = jnp.zeros((MOBA_BLOCK, MOBA_BLOCK), F32)
        for k in range(N_BUCKETS):
            b = jnp.where(bkt == k, tbl_ref[k, h], b)
        b = (b - far) * LOG2E
        if t == 0:
            b = jnp.where(dist >= 0, b, MASK_NEG)
        o_ref[1 + t] = b


def _bias_tiles(rel_table):
    tile_shape = (N_BIAS_TILES, MOBA_BLOCK, MOBA_BLOCK)
    return pl.pallas_call(
        _bias_kernel,
        out_shape=jax.ShapeDtypeStruct((N_HEADS,) + tile_shape, F32),
        grid=(N_HEADS,),
        in_specs=[pl.BlockSpec(memory_space=pltpu.SMEM)],
        out_specs=pl.BlockSpec((None,) + tile_shape, lambda h: (h, 0, 0, 0)),
        compiler_params=_params("arbitrary"),
        name="t5_bias_tiles",
    )(rel_table)


def _norm_kernel(x_ref, g_ref, o_ref):
    o_ref[...] = _rms(x_ref[...], g_ref[...]).astype(o_ref.dtype)


def _norm(x2, g3, layer):
    m_rows = x2.shape[0]
    tm = 512
    return pl.pallas_call(
        _norm_kernel,
        out_shape=jax.ShapeDtypeStruct((m_rows, D_MODEL), BF16),
        grid=(m_rows // tm,),
        in_specs=[pl.BlockSpec((tm, D_MODEL), lambda m: (m, 0)),
                  pl.BlockSpec((None, 1, D_MODEL), lambda m: (layer, 0, 0))],
        out_specs=pl.BlockSpec((tm, D_MODEL), lambda m: (m, 0)),
        compiler_params=_params("parallel"),
        name="rms_norm",
    )(x2, g3)


def _stationary_weight_matmul(w_ref, wbf_ref, lhs_ref):
    @pl.when(pl.program_id(1) == 0)
    def _():
        wbf_ref[...] = w_ref[...].astype(BF16)

    return jnp.dot(lhs_ref[...], wbf_ref[...], preferred_element_type=F32)


def _ride_along_cast_specs(weights, layer, n_steps, step_index):
    in_specs, out_specs, out_shapes = [], [], []
    for w in weights:
        rows, cols = w.shape[1] // n_steps, w.shape[2]
        assert rows * n_steps == w.shape[1] and rows % (2 * SUBLANES) == 0
        in_specs.append(pl.BlockSpec((None, rows, cols), lambda *g: (layer, step_index(*g), 0)))
        out_specs.append(pl.BlockSpec((rows, cols), lambda *g: (step_index(*g), 0)))
        out_shapes.append(jax.ShapeDtypeStruct(w.shape[1:], BF16))
    return in_specs, out_specs, out_shapes


def _ride_along_cast(src_refs, dst_refs):
    for src, dst in zip(src_refs, dst_refs):
        dst[...] = src[...].astype(dst.dtype)


def _in_proj_kernel(h_ref, w_ref, o_ref, wbf_ref):
    o_ref[...] = _stationary_weight_matmul(w_ref, wbf_ref, h_ref).astype(o_ref.dtype)


def _in_proj(h, w_in, layer):
    m_rows = h.shape[0]
    tm, tn = 1024, 1024
    return pl.pallas_call(
        _in_proj_kernel,
        out_shape=jax.ShapeDtypeStruct((m_rows, N_IN), BF16),
        grid=(N_IN // tn, m_rows // tm),
        in_specs=[
            pl.BlockSpec((tm, D_MODEL), lambda n, m: (m, 0)),
            pl.BlockSpec((None, D_MODEL, tn), lambda n, m: (layer, 0, n)),
        ],
        out_specs=pl.BlockSpec((tm, tn), lambda n, m: (m, n)),
        scratch_shapes=[pltpu.VMEM((D_MODEL, tn), BF16)],
        compiler_params=_params("parallel", "arbitrary"),
        name="in_proj",
    )(h, w_in)


ATTN_GROUP = 4
ATTN_Q_BLOCKS = 4


def _for_each_group(n_groups, body):
    def pair(p, carry):
        body(2 * p)
        body(2 * p + 1)
        return carry

    lax.fori_loop(0, n_groups // 2, pair, 0)

    @pl.when(n_groups % 2 == 1)
    def _():
        body(n_groups - 1)


def _split_bf16(x, parts):
    pieces = []
    for _ in range(parts):
        piece = x.astype(BF16).astype(F32)
        pieces.append(piece)
        x = x - piece
    return pieces


def _attn_kernel(far_ref, q_ref, k_ref, v_ref, bias_ref, *rest, n_blk, n_cast):
    cast_in, (o_ref,), cast_out = rest[:n_cast], rest[n_cast:n_cast + 1], rest[n_cast + 1:2 * n_cast + 1]
    kmean_ref, qa_ref, s_ref, m128_ref, mb_ref, acc_ref = rest[2 * n_cast + 1:]
    _ride_along_cast(cast_in, cast_out)
    h = pl.program_id(1)
    blk = MOBA_BLOCK
    q_rows = ATTN_Q_BLOCKS * blk
    far_lanes = (n_blk, n_blk + N_FAR_PARTS)
    assert n_blk % SUBLANES == 0 and N_FAR_PARTS <= SUBLANES and n_blk + SUBLANES <= LANES
    assert n_blk % ATTN_GROUP == 0 and n_blk % ATTN_Q_BLOCKS == 0

    for j in range(n_blk):
        kj = k_ref[j * blk:(j + 1) * blk, :].astype(F32)
        kmean_ref[j:j + 1, :] = jnp.sum(kj, axis=0, keepdims=True) * (1.0 / blk)
    kmean_parts = jnp.concatenate(_split_bf16(kmean_ref[...], 3), axis=0).astype(BF16)

    far = jnp.full((SUBLANES, blk), far_ref[h], F32) * LOG2E
    row8 = lax.broadcasted_iota(jnp.int32, (SUBLANES, blk), 0)
    far_rows = jnp.zeros((SUBLANES, blk), F32)
    for i, piece in enumerate(_split_bf16(far, N_FAR_PARTS)):
        far_rows = jnp.where(row8 == i, piece, far_rows)

    eye = jnp.where(lax.broadcasted_iota(jnp.int32, (blk, blk), 0)
                    == lax.broadcasted_iota(jnp.int32, (blk, blk), 1), 1.0, 0.0).astype(BF16)
    row = lax.broadcasted_iota(jnp.int32, (n_blk, blk), 0)
    pad_rows = jnp.zeros((LANES - n_blk - SUBLANES, blk), F32)

    for c in range(n_blk):
        rows = slice(c * blk, (c + 1) * blk)
        qf = q_ref[rows, :].astype(F32) * ((HEAD_DIM ** -0.5) * LOG2E)
        if c > MOBA_TOPK:
            g3 = lax.dot_general(kmean_parts, q_ref[rows, :], NT_DIMS, preferred_element_type=F32)
            gate_t = g3[0:n_blk] + g3[n_blk:2 * n_blk] + g3[2 * n_blk:3 * n_blk]
            rank = jnp.zeros((n_blk, blk), F32)
            for i in range(c):
                gi = gate_t[i:i + 1, :]
                tie_first = jnp.where(row > i, 1.0, 0.0)
                rank = rank + jnp.where(gi > gate_t, 1.0, jnp.where(gi == gate_t, tie_first, 0.0))
            neg_t = jnp.where(rank < float(MOBA_TOPK), 0.0, MASK_NEG)
            neg_t = jnp.where(row == c, 0.0, neg_t)
        else:
            neg_t = jnp.zeros((n_blk, blk), F32)
        aug_t = jnp.concatenate([neg_t, far_rows, pad_rows], axis=0).astype(BF16)
        aug_q = lax.dot_general(eye, aug_t, NT_DIMS, preferred_element_type=F32)
        qa_ref[rows, 0:HEAD_DIM] = qf.astype(BF16)
        qa_ref[rows, HEAD_DIM:2 * HEAD_DIM] = aug_q.astype(BF16)

    lane = lax.broadcasted_iota(jnp.int32, (blk, LANES), 1)
    far_ones = jnp.where((lane >= far_lanes[0]) & (lane < far_lanes[1]), 1.0, 0.0)

    def half_max(s):
        return jnp.maximum(s[:, :LANES], s[:, LANES:])

    group_ones = jnp.ones((ATTN_GROUP * blk, LANES), BF16)

    def q_tile(t, carry):
        qb0 = t * ATTN_Q_BLOCKS
        qrow = pl.multiple_of(t * q_rows, q_rows)
        n_groups = (qb0 + ATTN_Q_BLOCKS + ATTN_GROUP - 1) // ATTN_GROUP

        m128_ref[...] = jnp.full_like(m128_ref, MASK_NEG)

        def score_group(g):
            hm = None
            for u in range(ATTN_GROUP):
                j = g * ATTN_GROUP + u
                start = pl.multiple_of(j * blk, blk)
                k_aug = jnp.where(lane == j, 1.0, far_ones).astype(BF16)
                ka = jnp.concatenate([k_ref[pl.ds(start, blk), :], k_aug], axis=1)
                s = lax.dot_general(qa_ref[pl.ds(qrow, q_rows), :], ka, NT_DIMS,
                                    preferred_element_type=F32)
                s = jnp.concatenate(
                    [s[i * blk:(i + 1) * blk] + bias_ref[_bias_tile_index(qb0 + i, j)]
                     for i in range(ATTN_Q_BLOCKS)], axis=0)
                s_ref[j] = s
                hm = half_max(s) if hm is None else jnp.maximum(hm, half_max(s))
            m128_ref[...] = jnp.maximum(m128_ref[...], hm)

        _for_each_group(n_groups, score_group)

        m = jnp.max(m128_ref[...], axis=1, keepdims=True)
        mb_ref[...] = jnp.broadcast_to(m, (q_rows, LANES))

        acc_ref[...] = jnp.zeros_like(acc_ref)

        def pv_group(g):
            mb = mb_ref[...]
            mb2 = jnp.concatenate([mb, mb], axis=1)
            p = jnp.concatenate([jnp.exp2(s_ref[g * ATTN_GROUP + u] - mb2).astype(BF16)
                                 for u in range(ATTN_GROUP)], axis=1)
            start = pl.multiple_of(g * (ATTN_GROUP * blk), ATTN_GROUP * blk)
            va = jnp.concatenate([v_ref[pl.ds(start, ATTN_GROUP * blk), :], group_ones], axis=1)
            acc_ref[...] += jnp.dot(p, va, preferred_element_type=F32)

        _for_each_group(n_groups, pv_group)

        acc = acc_ref[...]
        o_ref[pl.ds(qrow, q_rows), :] = (acc[:, :HEAD_DIM] / acc[:, HEAD_DIM:]).astype(o_ref.dtype)
        return carry

    lax.fori_loop(0, n_blk // ATTN_Q_BLOCKS, q_tile, 0)


def _attention(proj3, bias_tiles, far_bias, cast_weights, layer):
    bsz, seq, _ = proj3.shape
    n_blk = seq // MOBA_BLOCK
    blk = MOBA_BLOCK
    q_rows = ATTN_Q_BLOCKS * blk

    def head_cols(first):
        return pl.BlockSpec((None, seq, HEAD_DIM), lambda b, h: (b, 0, first + h))

    cast_in, cast_out, cast_shapes = _ride_along_cast_specs(
        cast_weights, layer, bsz * N_HEADS, lambda b, h: b * N_HEADS + h)
    return pl.pallas_call(
        functools.partial(_attn_kernel, n_blk=n_blk, n_cast=len(cast_weights)),
        out_shape=[jax.ShapeDtypeStruct((bsz, seq, ATTN_W), BF16)] + cast_shapes,
        grid=(bsz, N_HEADS),
        in_specs=[
            pl.BlockSpec(memory_space=pltpu.SMEM),
            head_cols(0), head_cols(N_HEADS), head_cols(2 * N_HEADS),
            pl.BlockSpec((None, N_BIAS_TILES, blk, blk), lambda b, h: (h, 0, 0, 0)),
        ] + cast_in,
        out_specs=[head_cols(0)] + cast_out,
        scratch_shapes=[
            pltpu.VMEM((n_blk, HEAD_DIM), F32),
            pltpu.VMEM((seq, 2 * HEAD_DIM), BF16),
            pltpu.VMEM((n_blk, q_rows, blk), F32),
            pltpu.VMEM((q_rows, LANES), F32),
            pltpu.VMEM((q_rows, LANES), F32),
            pltpu.VMEM((q_rows, 2 * HEAD_DIM), F32),
        ],
        compiler_params=_params("parallel", "parallel"),
        name="moba_attention",
    )(far_bias, proj3, proj3, proj3, bias_tiles, *cast_weights)


def _gelu_tanh(x):
    return 0.5 * x * (1.0 + jnp.tanh(math.sqrt(2.0 / math.pi) * (x + 0.044715 * (x * x * x))))


def _softplus(z):
    return jnp.maximum(z, 0.0) + jnp.log1p(jnp.exp(-jnp.abs(z)))


def _mix_kernel(xr_ref, gr_ref, scb_ref, scc_ref, scx_ref,
                cw_ref, cb_ref, wg_ref, ba_ref, bx_ref, lam_ref, sw_ref,
                *rest, ts, n_cast):
    cast_in, (yb_ref, yc_ref), cast_out = rest[:n_cast], rest[n_cast:n_cast + 2], rest[n_cast + 2:2 * n_cast + 2]
    xbuf, pbuf, a_s, u_s, hcar = rest[2 * n_cast + 2:]
    _ride_along_cast(cast_in, cast_out)
    halo = SUBLANES
    first = pl.program_id(1) == 0

    @pl.when(first)
    def _():
        xbuf[0:halo, :] = jnp.zeros((halo, LRU_W), F32)
        pbuf[0:halo, :] = jnp.zeros((halo, SC_W), F32)
        hcar[...] = jnp.zeros_like(hcar)

    @pl.when(jnp.logical_not(first))
    def _():
        xbuf[0:halo, :] = xbuf[ts:ts + halo, :]
        pbuf[0:halo, :] = pbuf[ts:ts + halo, :]

    xbuf[halo:halo + ts, :] = xr_ref[...].astype(F32)
    xc = cb_ref[...] + cw_ref[LRU_CONV - 1:LRU_CONV, :] * xbuf[halo:halo + ts, :]
    for d in range(1, LRU_CONV):
        xc = xc + cw_ref[LRU_CONV - 1 - d:LRU_CONV - d, :] * xbuf[halo - d:halo - d + ts, :]
    xcb = xc.astype(BF16)
    neg_c_sp = -LRU_C * _softplus(-lam_ref[...])
    for g in range(LRU_BLOCKS):
        sl = slice(g * LRU_BW, (g + 1) * LRU_BW)
        ri = jnp.dot(xcb[:, sl], wg_ref[g], preferred_element_type=F32)
        r = jax.nn.sigmoid(ri[:, :LRU_BW] + ba_ref[:, sl])
        i = jax.nn.sigmoid(ri[:, LRU_BW:] + bx_ref[:, sl])
        log_a = neg_c_sp[:, sl] * r
        a = jnp.exp(log_a)
        mult = jnp.sqrt(-jnp.tanh(log_a) * (a * a + 1.0))
        a_s[:, sl] = a
        u_s[:, sl] = mult * i * xc[:, sl]

    rowi = lax.broadcasted_iota(jnp.int32, (SUBLANES, LRU_W), 0)

    def scan_body(t, hprev):
        r0 = pl.multiple_of(t * SUBLANES, SUBLANES)
        a = a_s[pl.ds(r0, SUBLANES), :]
        u = u_s[pl.ds(r0, SUBLANES), :]
        for sh in (1, 2, 4):
            a_sh = pltpu.roll(a, sh, axis=0)
            u_sh = pltpu.roll(u, sh, axis=0)
            ok = rowi >= sh
            u = jnp.where(ok, a * u_sh + u, u)
            a = jnp.where(ok, a * a_sh, a)
        hrow = a * hprev + u
        u_s[pl.ds(r0, SUBLANES), :] = hrow
        return jnp.broadcast_to(hrow[SUBLANES - 1:SUBLANES, :], (SUBLANES, LRU_W))

    hcar[...] = lax.fori_loop(0, ts // SUBLANES, scan_body, hcar[...], unroll=4)
    yb_ref[...] = (u_s[...] * _gelu_tanh(gr_ref[...].astype(F32))).astype(yb_ref.dtype)

    pbuf[halo:halo + ts, :] = scc_ref[...].astype(F32) * scx_ref[...].astype(F32)
    conv = sw_ref[SC_CONV - 1:SC_CONV, :] * pbuf[halo:halo + ts, :]
    for d in range(1, SC_CONV):
        conv = conv + sw_ref[SC_CONV - 1 - d:SC_CONV - d, :] * pbuf[halo - d:halo - d + ts, :]
    yc_ref[...] = (scb_ref[...].astype(F32) * conv).astype(yc_ref.dtype)


def _mixers(proj3, cw, cb3, wg_bf, ba3, bx3, lam3, sw, cast_weights, layer):
    bsz, seq, _ = proj3.shape
    ts = 512
    n_seq = seq // ts

    def col(c):
        return pl.BlockSpec((None, ts, LRU_W), lambda b, s: (b, s, c))

    def vec(width):
        return pl.BlockSpec((None, 1, width), lambda b, s: (layer, 0, 0))

    cast_in, cast_out, cast_shapes = _ride_along_cast_specs(
        cast_weights, layer, bsz * n_seq, lambda b, s: b * n_seq + s)
    return pl.pallas_call(
        functools.partial(_mix_kernel, ts=ts, n_cast=len(cast_weights)),
        out_shape=[jax.ShapeDtypeStruct((bsz, seq, LRU_W), BF16),
                   jax.ShapeDtypeStruct((bsz, seq, SC_W), BF16)] + cast_shapes,
        grid=(bsz, n_seq),
        in_specs=[
            col(COL_XR), col(COL_GR), col(COL_SCB), col(COL_SCC), col(COL_SCX),
            pl.BlockSpec((None, LRU_CONV, LRU_W), lambda b, s: (layer, 0, 0)),
            vec(LRU_W),
            pl.BlockSpec((None, LRU_BLOCKS, LRU_BW, 2 * LRU_BW), lambda b, s: (layer, 0, 0, 0)),
            vec(LRU_W), vec(LRU_W), vec(LRU_W),
            pl.BlockSpec((None, SC_CONV, SC_W), lambda b, s: (layer, 0, 0)),
        ] + cast_in,
        out_specs=[pl.BlockSpec((None, ts, LRU_W), lambda b, s: (b, s, 0)),
                   pl.BlockSpec((None, ts, SC_W), lambda b, s: (b, s, 0))] + cast_out,
        scratch_shapes=[
            pltpu.VMEM((SUBLANES + ts, LRU_W), F32),
            pltpu.VMEM((SUBLANES + ts, SC_W), F32),
            pltpu.VMEM((ts, LRU_W), F32),
            pltpu.VMEM((ts, LRU_W), F32),
            pltpu.VMEM((SUBLANES, LRU_W), F32),
        ],
        compiler_params=_params("parallel", "arbitrary"),
        name="lru_shortconv",
    )(proj3, proj3, proj3, proj3, proj3, cw, cb3, wg_bf, ba3, bx3, lam3, sw, *cast_weights)


def _merge_out_kernel(ya_ref, yb_ref, yc_ref, ga_ref, gb_ref, gc_ref, gbias_ref, x_ref,
                      wa_ref, wb_ref, wc_ref, wo_ref, o_ref):
    def branch(y_ref, w_ref, g_ref, k):
        y = jnp.dot(y_ref[...], w_ref[...], preferred_element_type=F32)
        gate = jax.nn.sigmoid(g_ref[...].astype(F32) + gbias_ref[:, k * D_MODEL:(k + 1) * D_MODEL])
        return gate * y

    m = branch(ya_ref, wa_ref, ga_ref, 0) + branch(yb_ref, wb_ref, gb_ref, 1)
    m = m + branch(yc_ref, wc_ref, gc_ref, 2)
    o_ref[...] = x_ref[...] + jnp.dot(m.astype(BF16), wo_ref[...], preferred_element_type=F32)


def _merge_out(ya, yb, yc, proj, gbias3, x2, wa_bf, wb_bf, wc_bf, wo_bf, layer):
    m_rows = ya.shape[0]
    tm = 256
    gate_tile = COL_GATES * 1024 // D_MODEL

    def rows(width):
        return pl.BlockSpec((tm, width), lambda m: (m, 0))

    def gate(k):
        return pl.BlockSpec((tm, D_MODEL), lambda m: (m, gate_tile + k))

    def resident(w):
        return pl.BlockSpec(w.shape, lambda m: (0, 0), pipeline_mode=pl.Buffered(1))

    return pl.pallas_call(
        _merge_out_kernel,
        out_shape=jax.ShapeDtypeStruct((m_rows, D_MODEL), F32),
        grid=(m_rows // tm,),
        in_specs=[
            rows(ATTN_W), rows(LRU_W), rows(SC_W), gate(0), gate(1), gate(2),
            pl.BlockSpec((None, 1, N_BRANCH * D_MODEL), lambda m: (layer, 0, 0)),
            rows(D_MODEL),
            resident(wa_bf), resident(wb_bf), resident(wc_bf), resident(wo_bf),
        ],
        out_specs=rows(D_MODEL),
        compiler_params=_params("parallel"),
        name="merge_out_proj",
    )(ya, yb, yc, proj, proj, proj, gbias3, x2, wa_bf, wb_bf, wc_bf, wo_bf)


def _mlp_kernel(x_ref, g_ref, wu_ref, wd_ref, ng_ref, *rest, last_layer):
    if last_layer:
        y_ref, h_ref, acc_ref = rest
    else:
        x_out_ref, h_next_ref, h_ref, acc_ref = rest
    f = pl.program_id(1)

    @pl.when(f == 0)
    def _():
        h_ref[...] = _rms(x_ref[...], g_ref[...]).astype(BF16)
        acc_ref[...] = jnp.zeros_like(acc_ref)

    up = jnp.dot(h_ref[...], wu_ref[...], preferred_element_type=F32)
    act = jnp.square(jnp.maximum(up, 0.0)).astype(BF16)
    acc_ref[...] += jnp.dot(act, wd_ref[...], preferred_element_type=F32)

    @pl.when(f == pl.num_programs(1) - 1)
    def _():
        x_new = x_ref[...] + acc_ref[...]
        normed = _rms(x_new, ng_ref[...])
        if last_layer:
            y_ref[...] = normed
        else:
            x_out_ref[...] = x_new
            h_next_ref[...] = normed.astype(h_next_ref.dtype)


def _mlp(x2, g3, wu_bf, wd_bf, next_g3, next_index, layer, last_layer):
    m_rows = x2.shape[0]
    tm, tf = 512, 1024
    row_spec = pl.BlockSpec((tm, D_MODEL), lambda m, f: (m, 0))
    x_shape = jax.ShapeDtypeStruct((m_rows, D_MODEL), F32)
    if last_layer:
        out_shape, out_specs = x_shape, row_spec
    else:
        out_shape = (x_shape, jax.ShapeDtypeStruct((m_rows, D_MODEL), BF16))
        out_specs = (row_spec, row_spec)
    return pl.pallas_call(
        functools.partial(_mlp_kernel, last_layer=last_layer),
        out_shape=out_shape,
        grid=(m_rows // tm, D_FF // tf),
        in_specs=[
            row_spec,
            pl.BlockSpec((None, 1, D_MODEL), lambda m, f: (layer, 0, 0)),
            pl.BlockSpec((D_MODEL, tf), lambda m, f: (0, f)),
            pl.BlockSpec((tf, D_MODEL), lambda m, f: (f, 0)),
            pl.BlockSpec((None, 1, D_MODEL), lambda m, f: (next_index, 0, 0)),
        ],
        out_specs=out_specs,
        scratch_shapes=[pltpu.VMEM((tm, D_MODEL), BF16), pltpu.VMEM((tm, D_MODEL), F32)],
        compiler_params=_params("parallel", "arbitrary"),
        name="mlp",
    )(x2, g3, wu_bf, wd_bf, next_g3)


def kernel(x, norm_mix_g, w_in, gate_b, rel_table, w_attn_o, lru_conv_w, lru_conv_b,
           lru_wa, lru_ba, lru_wx, lru_bx, lru_lambda, w_lru_o, sc_conv_w, w_sc_o,
           w_out, norm_mlp_g, w_mlp_up, w_mlp_down, final_g):
    bsz, seq, d_model = x.shape
    assert d_model == D_MODEL and seq % MOBA_BLOCK == 0
    assert w_in.shape == (DEPTH, D_MODEL, N_IN)
    m_rows = bsz * seq

    def vec3(v):
        return v.reshape(v.shape[0], 1, v.shape[1])

    w_gates_bf = jnp.concatenate([lru_wa, lru_wx], axis=-1).astype(BF16)

    bias_tiles = _bias_tiles(rel_table)
    far_bias = rel_table[N_BUCKETS - 1]
    mix_g3 = vec3(norm_mix_g)
    mlp_g3 = vec3(norm_mlp_g)
    final_g3 = final_g.reshape(1, 1, D_MODEL)

    x2 = x.reshape(m_rows, D_MODEL)
    h = _norm(x2, mix_g3, 0)
    for layer in range(DEPTH):
        proj = _in_proj(h, w_in, layer)
        proj3 = proj.reshape(bsz, seq, N_IN)
        ya, wa_bf, wb_bf, wc_bf, wo_bf, wu_bf = _attention(
            proj3, bias_tiles, far_bias, (w_attn_o, w_lru_o, w_sc_o, w_out, w_mlp_up), layer)
        yb, yc, wd_bf = _mixers(
            proj3, lru_conv_w, vec3(lru_conv_b), w_gates_bf, vec3(lru_ba), vec3(lru_bx),
            vec3(lru_lambda), sc_conv_w, (w_mlp_down,), layer)
        x2 = _merge_out(ya.reshape(m_rows, ATTN_W), yb.reshape(m_rows, LRU_W),
                        yc.reshape(m_rows, SC_W), proj, vec3(gate_b), x2,
                        wa_bf, wb_bf, wc_bf, wo_bf, layer)
        if layer < DEPTH - 1:
            x2, h = _mlp(x2, mlp_g3, wu_bf, wd_bf, mix_g3, layer + 1, layer, False)
        else:
            x2 = _mlp(x2, mlp_g3, wu_bf, wd_bf, final_g3, 0, layer, True)
    return x2.reshape(bsz, seq, D_MODEL)
```

```python
import functools
import math

import jax
import jax.numpy as jnp
from jax import lax
from jax.experimental import pallas as pl
from jax.experimental.pallas import tpu as pltpu

D_MODEL = 2048
DEPTH = 4
N_HEADS = 8
HEAD_DIM = 128
ATTN_W = N_HEADS * HEAD_DIM
MOBA_BLOCK = 256
MOBA_TOPK = 3
N_BUCKETS = 32
MAX_DISTANCE = 128
LRU_W = D_MODEL // 2
LRU_BLOCKS = 8
LRU_BW = LRU_W // LRU_BLOCKS
LRU_CONV = 4
LRU_C = 8.0
SC_W = D_MODEL // 2
SC_CONV = 3
D_FF = 4 * D_MODEL
N_BRANCH = 3
EPS = 1e-6
N_IN = 3 * ATTN_W + 2 * LRU_W + 3 * SC_W + N_BRANCH * D_MODEL

COL_XR, COL_GR, COL_SCB, COL_SCC, COL_SCX = 3, 4, 5, 6, 7
COL_GATES = 8

LANES = 128
SUBLANES = 8
MASK_NEG = -1e30
LOG2E = math.log2(math.e)
N_FAR_PARTS = 3
VMEM_LIMIT = 56 * 1024 * 1024

F32 = jnp.float32
BF16 = jnp.bfloat16
NT_DIMS = (((1,), (1,)), ((), ()))


def _params(*sem):
    return pltpu.CompilerParams(dimension_semantics=sem, vmem_limit_bytes=VMEM_LIMIT)


def _rms(x, g):
    ms = jnp.mean(x * x, axis=-1, keepdims=True)
    return x * lax.rsqrt(ms + EPS) * g


N_BIAS_TILES = 4


def _bias_tile_index(q_blk, k_blk):
    return jnp.clip(q_blk - k_blk, -1, N_BIAS_TILES - 2) + 1


def _bias_kernel(tbl_ref, o_ref):
    h = pl.program_id(0)
    far = tbl_ref[N_BUCKETS - 1, h]
    r = lax.broadcasted_iota(jnp.int32, (MOBA_BLOCK, MOBA_BLOCK), 0)
    c = lax.broadcasted_iota(jnp.int32, (MOBA_BLOCK, MOBA_BLOCK), 1)
    max_exact = N_BUCKETS // 2
    o_ref[0] = jnp.full((MOBA_BLOCK, MOBA_BLOCK), MASK_NEG, F32)
    o_ref[3] = jnp.zeros((MOBA_BLOCK, MOBA_BLOCK), F32)
    for t in range(2):
        dist = r - c + t * MOBA_BLOCK
        n = jnp.maximum(dist, 0)
        nf = jnp.maximum(n, 1).astype(F32)
        large = max_exact + (jnp.log(nf / max_exact) / math.log(MAX_DISTANCE / max_exact)
                             * (N_BUCKETS - max_exact)).astype(jnp.int32)
        large = jnp.minimum(large, N_BUCKETS - 1)
        bkt = jnp.where(n < max_exact, n, large)
        b = jnp.zeros((MOBA_BLOCK, MOBA_BLOCK), F32)
        for k in range(N_BUCKETS):
            b = jnp.where(bkt == k, tbl_ref[k, h], b)
        b = (b - far) * LOG2E
        if t == 0:
            b = jnp.where(dist >= 0, b, MASK_NEG)
        o_ref[1 + t] = b


def _bias_tiles(rel_table):
    tile_shape = (N_BIAS_TILES, MOBA_BLOCK, MOBA_BLOCK)
    return pl.pallas_call(
        _bias_kernel,
        out_shape=jax.ShapeDtypeStruct((N_HEADS,) + tile_shape, F32),
        grid=(N_HEADS,),
        in_specs=[pl.BlockSpec(memory_space=pltpu.SMEM)],
        out_specs=pl.BlockSpec((None,) + tile_shape, lambda h: (h, 0, 0, 0)),
        compiler_params=_params("arbitrary"),
        name="t5_bias_tiles",
    )(rel_table)


def _norm_kernel(x_ref, g_ref, o_ref):
    o_ref[...] = _rms(x_ref[...], g_ref[...]).astype(o_ref.dtype)


def _norm(x2, g3, layer):
    m_rows = x2.shape[0]
    tm = 512
    return pl.pallas_call(
        _norm_kernel,
        out_shape=jax.ShapeDtypeStruct((m_rows, D_MODEL), BF16),
        grid=(m_rows // tm,),
        in_specs=[pl.BlockSpec((tm, D_MODEL), lambda m: (m, 0)),
                  pl.BlockSpec((None, 1, D_MODEL), lambda m: (layer, 0, 0))],
        out_specs=pl.BlockSpec((tm, D_MODEL), lambda m: (m, 0)),
        compiler_params=_params("parallel"),
        name="rms_norm",
    )(x2, g3)


def _stationary_weight_matmul(w_ref, wbf_ref, lhs_ref):
    @pl.when(pl.program_id(1) == 0)
    def _():
        wbf_ref[...] = w_ref[...].astype(BF16)

    return jnp.dot(lhs_ref[...], wbf_ref[...], preferred_element_type=F32)


def _ride_along_cast_specs(weights, layer, n_steps, step_index):
    in_specs, out_specs, out_shapes = [], [], []
    for w in weights:
        rows, cols = w.shape[1] // n_steps, w.shape[2]
        assert rows * n_steps == w.shape[1] and rows % (2 * SUBLANES) == 0
        in_specs.append(pl.BlockSpec((None, rows, cols), lambda *g: (layer, step_index(*g), 0)))
        out_specs.append(pl.BlockSpec((rows, cols), lambda *g: (step_index(*g), 0)))
        out_shapes.append(jax.ShapeDtypeStruct(w.shape[1:], BF16))
    return in_specs, out_specs, out_shapes


def _ride_along_cast(src_refs, dst_refs):
    for src, dst in zip(src_refs, dst_refs):
        dst[...] = src[...].astype(dst.dtype)


def _lane_concat(slab_ref):
    return jnp.concatenate([slab_ref[i] for i in range(slab_ref.shape[0])], axis=1)


def _in_proj_kernel(h_ref, w_ref, o_ref, wbf_ref):
    res = _stationary_weight_matmul(w_ref, wbf_ref, h_ref)
    for i in range(o_ref.shape[0]):
        o_ref[i] = res[:, i * LANES:(i + 1) * LANES].astype(o_ref.dtype)


def _in_proj(h, w_in, layer):
    m_rows = h.shape[0]
    tm, tn = 1024, 1024
    return pl.pallas_call(
        _in_proj_kernel,
        out_shape=jax.ShapeDtypeStruct((N_IN // LANES, m_rows, LANES), BF16),
        grid=(N_IN // tn, m_rows // tm),
        in_specs=[
            pl.BlockSpec((tm, D_MODEL), lambda n, m: (m, 0)),
            pl.BlockSpec((None, D_MODEL, tn), lambda n, m: (layer, 0, n)),
        ],
        out_specs=pl.BlockSpec((tn // LANES, tm, LANES), lambda n, m: (n, m, 0)),
        scratch_shapes=[pltpu.VMEM((D_MODEL, tn), BF16)],
        compiler_params=_params("parallel", "arbitrary"),
        name="in_proj",
    )(h, w_in)


ATTN_GROUP = 4
ATTN_Q_BLOCKS = 4


def _for_each_group(n_groups, body):
    def pair(p, carry):
        body(2 * p)
        body(2 * p + 1)
        return carry

    lax.fori_loop(0, n_groups // 2, pair, 0)

    @pl.when(n_groups % 2 == 1)
    def _():
        body(n_groups - 1)


def _split_bf16(x, parts):
    pieces = []
    for _ in range(parts):
        piece = x.astype(BF16).astype(F32)
        pieces.append(piece)
        x = x - piece
    return pieces


def _attn_kernel(far_ref, q_ref, k_ref, v_ref, bias_ref, *rest, n_blk, n_cast):
    cast_in, (o_ref,), cast_out = rest[:n_cast], rest[n_cast:n_cast + 1], rest[n_cast + 1:2 * n_cast + 1]
    kmean_ref, qa_ref, s_ref, m128_ref, mb_ref, acc_ref = rest[2 * n_cast + 1:]
    _ride_along_cast(cast_in, cast_out)
    h = pl.program_id(1)
    blk = MOBA_BLOCK
    q_rows = ATTN_Q_BLOCKS * blk
    far_lanes = (n_blk, n_blk + N_FAR_PARTS)
    assert n_blk % SUBLANES == 0 and N_FAR_PARTS <= SUBLANES and n_blk + SUBLANES <= LANES
    assert n_blk % ATTN_GROUP == 0 and n_blk % ATTN_Q_BLOCKS == 0

    for j in range(n_blk):
        kj = k_ref[j * blk:(j + 1) * blk, :].astype(F32)
        kmean_ref[j:j + 1, :] = jnp.sum(kj, axis=0, keepdims=True) * (1.0 / blk)
    kmean_parts = jnp.concatenate(_split_bf16(kmean_ref[...], 3), axis=0).astype(BF16)

    far = jnp.full((SUBLANES, blk), far_ref[h], F32) * LOG2E
    row8 = lax.broadcasted_iota(jnp.int32, (SUBLANES, blk), 0)
    far_rows = jnp.zeros((SUBLANES, blk), F32)
    for i, piece in enumerate(_split_bf16(far, N_FAR_PARTS)):
        far_rows = jnp.where(row8 == i, piece, far_rows)

    eye = jnp.where(lax.broadcasted_iota(jnp.int32, (blk, blk), 0)
                    == lax.broadcasted_iota(jnp.int32, (blk, blk), 1), 1.0, 0.0).astype(BF16)
    row = lax.broadcasted_iota(jnp.int32, (n_blk, blk), 0)
    pad_rows = jnp.zeros((LANES - n_blk - SUBLANES, blk), F32)

    for c in range(n_blk):
        rows = slice(c * blk, (c + 1) * blk)
        qf = q_ref[rows, :].astype(F32) * ((HEAD_DIM ** -0.5) * LOG2E)
        if c > MOBA_TOPK:
            g3 = lax.dot_general(kmean_parts, q_ref[rows, :], NT_DIMS, preferred_element_type=F32)
            gate_t = g3[0:n_blk] + g3[n_blk:2 * n_blk] + g3[2 * n_blk:3 * n_blk]
            rank = jnp.zeros((n_blk, blk), F32)
            for i in range(c):
                gi = gate_t[i:i + 1, :]
                tie_first = jnp.where(row > i, 1.0, 0.0)
                rank = rank + jnp.where(gi > gate_t, 1.0, jnp.where(gi == gate_t, tie_first, 0.0))
            neg_t = jnp.where(rank < float(MOBA_TOPK), 0.0, MASK_NEG)
            neg_t = jnp.where(row == c, 0.0, neg_t)
        else:
            neg_t = jnp.zeros((n_blk, blk), F32)
        aug_t = jnp.concatenate([neg_t, far_rows, pad_rows], axis=0).astype(BF16)
        aug_q = lax.dot_general(eye, aug_t, NT_DIMS, preferred_element_type=F32)
        qa_ref[rows, 0:HEAD_DIM] = qf.astype(BF16)
        qa_ref[rows, HEAD_DIM:2 * HEAD_DIM] = aug_q.astype(BF16)

    lane = lax.broadcasted_iota(jnp.int32, (blk, LANES), 1)
    far_ones = jnp.where((lane >= far_lanes[0]) & (lane < far_lanes[1]), 1.0, 0.0)

    def half_max(s):
        return jnp.maximum(s[:, :LANES], s[:, LANES:])

    group_ones = jnp.ones((ATTN_GROUP * blk, LANES), BF16)

    def q_tile(t, carry):
        qb0 = t * ATTN_Q_BLOCKS
        qrow = pl.multiple_of(t * q_rows, q_rows)
        n_groups = (qb0 + ATTN_Q_BLOCKS + ATTN_GROUP - 1) // ATTN_GROUP

        m128_ref[...] = jnp.full_like(m128_ref, MASK_NEG)

        def score_group(g):
            hm = None
            for u in range(ATTN_GROUP):
                j = g * ATTN_GROUP + u
                start = pl.multiple_of(j * blk, blk)
                k_aug = jnp.where(lane == j, 1.0, far_ones).astype(BF16)
                ka = jnp.concatenate([k_ref[pl.ds(start, blk), :], k_aug], axis=1)
                s = lax.dot_general(qa_ref[pl.ds(qrow, q_rows), :], ka, NT_DIMS,
                                    preferred_element_type=F32)
                s = jnp.concatenate(
                    [s[i * blk:(i + 1) * blk] + bias_ref[_bias_tile_index(qb0 + i, j)]
                     for i in range(ATTN_Q_BLOCKS)], axis=0)
                s_ref[j] = s
                hm = half_max(s) if hm is None else jnp.maximum(hm, half_max(s))
            m128_ref[...] = jnp.maximum(m128_ref[...], hm)

        _for_each_group(n_groups, score_group)

        m = jnp.max(m128_ref[...], axis=1, keepdims=True)
        mb_ref[...] = jnp.broadcast_to(m, (q_rows, LANES))

        acc_ref[...] = jnp.zeros_like(acc_ref)

        def pv_group(g):
            mb = mb_ref[...]
            mb2 = jnp.concatenate([mb, mb], axis=1)
            p = jnp.concatenate([jnp.exp2(s_ref[g * ATTN_GROUP + u] - mb2).astype(BF16)
                                 for u in range(ATTN_GROUP)], axis=1)
            start = pl.multiple_of(g * (ATTN_GROUP * blk), ATTN_GROUP * blk)
            va = jnp.concatenate([v_ref[pl.ds(start, ATTN_GROUP * blk), :], group_ones], axis=1)
            acc_ref[...] += jnp.dot(p, va, preferred_element_type=F32)

        _for_each_group(n_groups, pv_group)

        acc = acc_ref[...]
        o_ref[pl.ds(qrow, q_rows), :] = (acc[:, :HEAD_DIM] / acc[:, HEAD_DIM:]).astype(o_ref.dtype)
        return carry

    lax.fori_loop(0, n_blk // ATTN_Q_BLOCKS, q_tile, 0)


def _attention(proj3, bias_tiles, far_bias, cast_weights, layer):
    _, bsz, seq, _ = proj3.shape
    n_blk = seq // MOBA_BLOCK
    blk = MOBA_BLOCK
    q_rows = ATTN_Q_BLOCKS * blk

    def head_cols(first):
        return pl.BlockSpec((None, None, seq, HEAD_DIM), lambda b, h: (first + h, b, 0, 0))

    cast_in, cast_out, cast_shapes = _ride_along_cast_specs(
        cast_weights, layer, bsz * N_HEADS, lambda b, h: b * N_HEADS + h)
    return pl.pallas_call(
        functools.partial(_attn_kernel, n_blk=n_blk, n_cast=len(cast_weights)),
        out_shape=[jax.ShapeDtypeStruct((N_HEADS, bsz, seq, HEAD_DIM), BF16)] + cast_shapes,
        grid=(bsz, N_HEADS),
        in_specs=[
            pl.BlockSpec(memory_space=pltpu.SMEM),
            head_cols(0), head_cols(N_HEADS), head_cols(2 * N_HEADS),
            pl.BlockSpec((None, N_BIAS_TILES, blk, blk), lambda b, h: (h, 0, 0, 0)),
        ] + cast_in,
        out_specs=[head_cols(0)] + cast_out,
        scratch_shapes=[
            pltpu.VMEM((n_blk, HEAD_DIM), F32),
            pltpu.VMEM((seq, 2 * HEAD_DIM), BF16),
            pltpu.VMEM((n_blk, q_rows, blk), F32),
            pltpu.VMEM((q_rows, LANES), F32),
            pltpu.VMEM((q_rows, LANES), F32),
            pltpu.VMEM((q_rows, 2 * HEAD_DIM), F32),
        ],
        compiler_params=_params("parallel", "parallel"),
        name="moba_attention",
    )(far_bias, proj3, proj3, proj3, bias_tiles, *cast_weights)


def _gelu_tanh(x):
    return 0.5 * x * (1.0 + jnp.tanh(math.sqrt(2.0 / math.pi) * (x + 0.044715 * (x * x * x))))


def _softplus(z):
    return jnp.maximum(z, 0.0) + jnp.log1p(jnp.exp(-jnp.abs(z)))


def _mix_kernel(xr_ref, gr_ref, scb_ref, scc_ref, scx_ref,
                cw_ref, cb_ref, wg_ref, ba_ref, bx_ref, lam_ref, sw_ref,
                *rest, ts, n_cast):
    cast_in, (yb_ref, yc_ref), cast_out = rest[:n_cast], rest[n_cast:n_cast + 2], rest[n_cast + 2:2 * n_cast + 2]
    xbuf, pbuf, a_s, u_s, hcar = rest[2 * n_cast + 2:]
    _ride_along_cast(cast_in, cast_out)
    halo = SUBLANES
    first = pl.program_id(1) == 0

    @pl.when(first)
    def _():
        xbuf[0:halo, :] = jnp.zeros((halo, LRU_W), F32)
        pbuf[0:halo, :] = jnp.zeros((halo, SC_W), F32)
        hcar[...] = jnp.zeros_like(hcar)

    @pl.when(jnp.logical_not(first))
    def _():
        xbuf[0:halo, :] = xbuf[ts:ts + halo, :]
        pbuf[0:halo, :] = pbuf[ts:ts + halo, :]

    xbuf[halo:halo + ts, :] = _lane_concat(xr_ref).astype(F32)
    xc = cb_ref[...] + cw_ref[LRU_CONV - 1:LRU_CONV, :] * xbuf[halo:halo + ts, :]
    for d in range(1, LRU_CONV):
        xc = xc + cw_ref[LRU_CONV - 1 - d:LRU_CONV - d, :] * xbuf[halo - d:halo - d + ts, :]
    xcb = xc.astype(BF16)
    neg_c_sp = -LRU_C * _softplus(-lam_ref[...])
    for g in range(LRU_BLOCKS):
        sl = slice(g * LRU_BW, (g + 1) * LRU_BW)
        ri = jnp.dot(xcb[:, sl], wg_ref[g], preferred_element_type=F32)
        r = jax.nn.sigmoid(ri[:, :LRU_BW] + ba_ref[:, sl])
        i = jax.nn.sigmoid(ri[:, LRU_BW:] + bx_ref[:, sl])
        log_a = neg_c_sp[:, sl] * r
        a = jnp.exp(log_a)
        mult = jnp.sqrt(-jnp.tanh(log_a) * (a * a + 1.0))
        a_s[:, sl] = a
        u_s[:, sl] = mult * i * xc[:, sl]

    rowi = lax.broadcasted_iota(jnp.int32, (SUBLANES, LRU_W), 0)

    def scan_body(t, hprev):
        r0 = pl.multiple_of(t * SUBLANES, SUBLANES)
        a = a_s[pl.ds(r0, SUBLANES), :]
        u = u_s[pl.ds(r0, SUBLANES), :]
        for sh in (1, 2, 4):
            a_sh = pltpu.roll(a, sh, axis=0)
            u_sh = pltpu.roll(u, sh, axis=0)
            ok = rowi >= sh
            u = jnp.where(ok, a * u_sh + u, u)
            a = jnp.where(ok, a * a_sh, a)
        hrow = a * hprev + u
        u_s[pl.ds(r0, SUBLANES), :] = hrow
        return jnp.broadcast_to(hrow[SUBLANES - 1:SUBLANES, :], (SUBLANES, LRU_W))

    hcar[...] = lax.fori_loop(0, ts // SUBLANES, scan_body, hcar[...], unroll=4)
    yb_ref[...] = (u_s[...] * _gelu_tanh(_lane_concat(gr_ref).astype(F32))).astype(yb_ref.dtype)

    pbuf[halo:halo + ts, :] = _lane_concat(scc_ref).astype(F32) * _lane_concat(scx_ref).astype(F32)
    conv = sw_ref[SC_CONV - 1:SC_CONV, :] * pbuf[halo:halo + ts, :]
    for d in range(1, SC_CONV):
        conv = conv + sw_ref[SC_CONV - 1 - d:SC_CONV - d, :] * pbuf[halo - d:halo - d + ts, :]
    yc_ref[...] = (_lane_concat(scb_ref).astype(F32) * conv).astype(yc_ref.dtype)


def _mixers(proj3, cw, cb3, wg_bf, ba3, bx3, lam3, sw, cast_weights, layer):
    _, bsz, seq, _ = proj3.shape
    ts = 512
    n_seq = seq // ts

    def col(c):
        return pl.BlockSpec((LRU_W // LANES, None, ts, LANES), lambda b, s: (c, b, s, 0))

    def vec(width):
        return pl.BlockSpec((None, 1, width), lambda b, s: (layer, 0, 0))

    cast_in, cast_out, cast_shapes = _ride_along_cast_specs(
        cast_weights, layer, bsz * n_seq, lambda b, s: b * n_seq + s)
    return pl.pallas_call(
        functools.partial(_mix_kernel, ts=ts, n_cast=len(cast_weights)),
        out_shape=[jax.ShapeDtypeStruct((bsz, seq, LRU_W), BF16),
                   jax.ShapeDtypeStruct((bsz, seq, SC_W), BF16)] + cast_shapes,
        grid=(bsz, n_seq),
        in_specs=[
            col(COL_XR), col(COL_GR), col(COL_SCB), col(COL_SCC), col(COL_SCX),
            pl.BlockSpec((None, LRU_CONV, LRU_W), lambda b, s: (layer, 0, 0)),
            vec(LRU_W),
            pl.BlockSpec((None, LRU_BLOCKS, LRU_BW, 2 * LRU_BW), lambda b, s: (layer, 0, 0, 0)),
            vec(LRU_W), vec(LRU_W), vec(LRU_W),
            pl.BlockSpec((None, SC_CONV, SC_W), lambda b, s: (layer, 0, 0)),
        ] + cast_in,
        out_specs=[pl.BlockSpec((None, ts, LRU_W), lambda b, s: (b, s, 0)),
                   pl.BlockSpec((None, ts, SC_W), lambda b, s: (b, s, 0))] + cast_out,
        scratch_shapes=[
            pltpu.VMEM((SUBLANES + ts, LRU_W), F32),
            pltpu.VMEM((SUBLANES + ts, SC_W), F32),
            pltpu.VMEM((ts, LRU_W), F32),
            pltpu.VMEM((ts, LRU_W), F32),
            pltpu.VMEM((SUBLANES, LRU_W), F32),
        ],
        compiler_params=_params("parallel", "arbitrary"),
        name="lru_shortconv",
    )(proj3, proj3, proj3, proj3, proj3, cw, cb3, wg_bf, ba3, bx3, lam3, sw, *cast_weights)


def _merge_out_kernel(ya_ref, yb_ref, yc_ref, ga_ref, gb_ref, gc_ref, gbias_ref, x_ref,
                      wa_ref, wb_ref, wc_ref, wo_ref, o_ref):
    def branch(y, w_ref, g_ref, k):
        y = jnp.dot(y, w_ref[...], preferred_element_type=F32)
        gate = jax.nn.sigmoid(_lane_concat(g_ref).astype(F32)
                              + gbias_ref[:, k * D_MODEL:(k + 1) * D_MODEL])
        return gate * y

    m = branch(_lane_concat(ya_ref), wa_ref, ga_ref, 0) + branch(yb_ref[...], wb_ref, gb_ref, 1)
    m = m + branch(yc_ref[...], wc_ref, gc_ref, 2)
    o_ref[...] = x_ref[...] + jnp.dot(m.astype(BF16), wo_ref[...], preferred_element_type=F32)


def _merge_out(ya, yb, yc, proj, gbias3, x2, wa_bf, wb_bf, wc_bf, wo_bf, layer):
    m_rows = yb.shape[0]
    tm = 256
    gate_slabs = D_MODEL // LANES
    gate_block0 = COL_GATES * 1024 // D_MODEL

    def rows(width):
        return pl.BlockSpec((tm, width), lambda m: (m, 0))

    def gate(k):
        return pl.BlockSpec((gate_slabs, tm, LANES), lambda m: (gate_block0 + k, m, 0))

    def resident(w):
        return pl.BlockSpec(w.shape, lambda m: (0, 0), pipeline_mode=pl.Buffered(1))

    return pl.pallas_call(
        _merge_out_kernel,
        out_shape=jax.ShapeDtypeStruct((m_rows, D_MODEL), F32),
        grid=(m_rows // tm,),
        in_specs=[
            pl.BlockSpec((N_HEADS, tm, HEAD_DIM), lambda m: (0, m, 0)),
            rows(LRU_W), rows(SC_W), gate(0), gate(1), gate(2),
            pl.BlockSpec((None, 1, N_BRANCH * D_MODEL), lambda m: (layer, 0, 0)),
            rows(D_MODEL),
            resident(wa_bf), resident(wb_bf), resident(wc_bf), resident(wo_bf),
        ],
        out_specs=rows(D_MODEL),
        compiler_params=_params("parallel"),
        name="merge_out_proj",
    )(ya, yb, yc, proj, proj, proj, gbias3, x2, wa_bf, wb_bf, wc_bf, wo_bf)


def _mlp_kernel(x_ref, g_ref, wu_ref, wd_ref, ng_ref, *rest, last_layer):
    if last_layer:
        y_ref, h_ref, acc_ref = rest
    else:
        x_out_ref, h_next_ref, h_ref, acc_ref = rest
    f = pl.program_id(1)

    @pl.when(f == 0)
    def _():
        h_ref[...] = _rms(x_ref[...], g_ref[...]).astype(BF16)
        acc_ref[...] = jnp.zeros_like(acc_ref)

    up = jnp.dot(h_ref[...], wu_ref[...], preferred_element_type=F32)
    act = jnp.square(jnp.maximum(up, 0.0)).astype(BF16)
    acc_ref[...] += jnp.dot(act, wd_ref[...], preferred_element_type=F32)

    @pl.when(f == pl.num_programs(1) - 1)
    def _():
        x_new = x_ref[...] + acc_ref[...]
        normed = _rms(x_new, ng_ref[...])
        if last_layer:
            y_ref[...] = normed
        else:
            x_out_ref[...] = x_new
            h_next_ref[...] = normed.astype(h_next_ref.dtype)


def _mlp(x2, g3, wu_bf, wd_bf, next_g3, next_index, layer, last_layer):
    m_rows = x2.shape[0]
    tm, tf = 512, 1024
    row_spec = pl.BlockSpec((tm, D_MODEL), lambda m, f: (m, 0))
    x_shape = jax.ShapeDtypeStruct((m_rows, D_MODEL), F32)
    if last_layer:
        out_shape, out_specs = x_shape, row_spec
    else:
        out_shape = (x_shape, jax.ShapeDtypeStruct((m_rows, D_MODEL), BF16))
        out_specs = (row_spec, row_spec)
    return pl.pallas_call(
        functools.partial(_mlp_kernel, last_layer=last_layer),
        out_shape=out_shape,
        grid=(m_rows // tm, D_FF // tf),
        in_specs=[
            row_spec,
            pl.BlockSpec((None, 1, D_MODEL), lambda m, f: (layer, 0, 0)),
            pl.BlockSpec((D_MODEL, tf), lambda m, f: (0, f)),
            pl.BlockSpec((tf, D_MODEL), lambda m, f: (f, 0)),
            pl.BlockSpec((None, 1, D_MODEL), lambda m, f: (next_index, 0, 0)),
        ],
        out_specs=out_specs,
        scratch_shapes=[pltpu.VMEM((tm, D_MODEL), BF16), pltpu.VMEM((tm, D_MODEL), F32)],
        compiler_params=_params("parallel", "arbitrary"),
        name="mlp",
    )(x2, g3, wu_bf, wd_bf, next_g3)


def kernel(x, norm_mix_g, w_in, gate_b, rel_table, w_attn_o, lru_conv_w, lru_conv_b,
           lru_wa, lru_ba, lru_wx, lru_bx, lru_lambda, w_lru_o, sc_conv_w, w_sc_o,
           w_out, norm_mlp_g, w_mlp_up, w_mlp_down, final_g):
    bsz, seq, d_model = x.shape
    assert d_model == D_MODEL and seq % MOBA_BLOCK == 0
    assert w_in.shape == (DEPTH, D_MODEL, N_IN)
    m_rows = bsz * seq

    def vec3(v):
        return v.reshape(v.shape[0], 1, v.shape[1])

    w_gates_bf = jnp.concatenate([lru_wa, lru_wx], axis=-1).astype(BF16)

    bias_tiles = _bias_tiles(rel_table)
    far_bias = rel_table[N_BUCKETS - 1]
    mix_g3 = vec3(norm_mix_g)
    mlp_g3 = vec3(norm_mlp_g)
    final_g3 = final_g.reshape(1, 1, D_MODEL)

    x2 = x.reshape(m_rows, D_MODEL)
    h = _norm(x2, mix_g3, 0)
    for layer in range(DEPTH):
        proj = _in_proj(h, w_in, layer)
        proj3 = proj.reshape(N_IN // LANES, bsz, seq, LANES)
        ya, wa_bf, wb_bf, wc_bf, wo_bf, wu_bf = _attention(
            proj3, bias_tiles, far_bias, (w_attn_o, w_lru_o, w_sc_o, w_out, w_mlp_up), layer)
        yb, yc, wd_bf = _mixers(
            proj3, lru_conv_w, vec3(lru_conv_b), w_gates_bf, vec3(lru_ba), vec3(lru_bx),
            vec3(lru_lambda), sc_conv_w, (w_mlp_down,), layer)
        x2 = _merge_out(ya.reshape(N_HEADS, m_rows, HEAD_DIM), yb.reshape(m_rows, LRU_W),
                        yc.reshape(m_rows, SC_W), proj, vec3(gate_b), x2,
                        wa_bf, wb_bf, wc_bf, wo_bf, layer)
        if layer < DEPTH - 1:
            x2, h = _mlp(x2, mlp_g3, wu_bf, wd_bf, mix_g3, layer + 1, layer, False)
        else:
            x2 = _mlp(x2, mlp_g3, wu_bf, wd_bf, final_g3, 0, layer, True)
    return x2.reshape(bsz, seq, D_MODEL)
```
